```python
import jax, jax.numpy as jnp
from jax import lax
import numpy as np

D_MODEL = 1024
BATCH = 8
SEQ = 8192
DEPTH = 1

CHUNK = 64
Q_BLOCK = 128
HEAD_DIM_RWKV = 64
RWKV_DIM = D_MODEL // 2
RWKV_HEADS = RWKV_DIM // HEAD_DIM_RWKV
DECAY_LORA = 64
ICLR_LORA = 64
GATE_LORA = 160
GN_EPS = 64e-5
MLA_HEADS = D_MODEL // 128
Q_LORA = 384
KV_LORA = 256
QK_NOPE = 64
QK_ROPE = 32
V_HEAD = 64
QK_HEAD = QK_NOPE + QK_ROPE
ROPE_THETA = 10000.0
D_FF = 2752
LN_EPS = 1e-5
RMS_EPS = 1e-6
ALPHA = (2.0 * DEPTH) ** 0.25
BETA = (8.0 * DEPTH) ** -0.25
SHIFT_COLS = 3 * RWKV_DIM + DECAY_LORA + ICLR_LORA + GATE_LORA
MLA_COLS = Q_LORA + KV_LORA + QK_ROPE
IN_COLS = SHIFT_COLS + MLA_COLS + 2 * D_MODEL

kernel_name = "hybrid_rwkv7_mla_macaron_deepnorm"


def layer_norm(x, g, b):
    xf = x.astype(jnp.float32)
    mu = jnp.mean(xf, axis=-1, keepdims=True)
    var = jnp.mean(jnp.square(xf - mu), axis=-1, keepdims=True)
    y = (xf - mu) * lax.rsqrt(var + LN_EPS)
    return (y * g.astype(jnp.float32) + b.astype(jnp.float32)).astype(x.dtype)


def rms_norm(x, g):
    xf = x.astype(jnp.float32)
    y = xf * lax.rsqrt(jnp.mean(jnp.square(xf), axis=-1, keepdims=True) + RMS_EPS)
    return (y * g.astype(jnp.float32)).astype(x.dtype)


def swiglu(x, w1, w3, w2):
    return (jax.nn.silu(x @ w1) * (x @ w3)) @ w2


def rope_tables(seq):
    inv_freq = ROPE_THETA ** (-jnp.arange(0, QK_ROPE, 2, dtype=jnp.float32) / QK_ROPE)
    ang = jnp.arange(seq, dtype=jnp.float32)[:, None] * inv_freq[None, :]
    return jnp.cos(ang), jnp.sin(ang)


def apply_rope(x, cos, sin):
    xf = x.astype(jnp.float32)
    x1, x2 = jnp.split(xf, 2, axis=-1)
    return jnp.concatenate([x1 * cos - x2 * sin, x2 * cos + x1 * sin], axis=-1).astype(x.dtype)


def rwkv7_mixer(p, mu_shift, w0, w_decay_up, a0, w_iclr_up, w_gate_up, k_k, k_a, r_k, gn_g, gn_b):
    bsz, seq, _ = p.shape
    h, n = RWKV_HEADS, HEAD_DIM_RWKV
    prev = jnp.pad(p[:, :-1], ((0, 0), (1, 0), (0, 0)))
    p = p + mu_shift * (prev - p)
    r, k, v, wd, ad, gd = jnp.split(
        p, [RWKV_DIM, 2 * RWKV_DIM, 3 * RWKV_DIM, 3 * RWKV_DIM + DECAY_LORA,
            3 * RWKV_DIM + DECAY_LORA + ICLR_LORA], axis=-1)
    w = -jax.nn.softplus(-(w0 + jnp.tanh(wd) @ w_decay_up)) - 0.5
    decay = jnp.exp(-jnp.exp(w.astype(jnp.float32)))
    a = jax.nn.sigmoid(a0 + ad @ w_iclr_up)
    g = jax.nn.sigmoid(gd) @ w_gate_up
    kk = (k * k_k).reshape(bsz, seq, h, n).astype(jnp.float32)
    kk = kk * lax.rsqrt(jnp.maximum(jnp.sum(kk * kk, axis=-1, keepdims=True), 1e-24))
    k = k * (1.0 + (a - 1.0) * k_a)

    def heads(t):
        return t.reshape(bsz, seq, h, n).astype(jnp.float32)

    r_h, k_h, v_h, a_h, w_h = heads(r), heads(k), heads(v), heads(a), heads(decay)

    def step(state, inp):
        r_t, w_t, k_t, v_t, kk_t, b_t = inp
        sa = jnp.einsum('bhvk,bhk->bhv', state, -kk_t)
        state = (state * w_t[:, :, None, :] + sa[..., None] * b_t[:, :, None, :]
                 + v_t[..., None] * k_t[:, :, None, :])
        return state, jnp.einsum('bhvk,bhk->bhv', state, r_t)

    xs = tuple(jnp.moveaxis(t, 1, 0) for t in (r_h, w_h, k_h, v_h, kk, kk * a_h))
    state0 = jnp.zeros((bsz, h, n, n), jnp.float32)
    _, y = lax.scan(step, state0, xs)
    y = jnp.moveaxis(y, 0, 1)
    mu = jnp.mean(y, axis=-1, keepdims=True)
    var = jnp.mean(jnp.square(y - mu), axis=-1, keepdims=True)
    y = ((y - mu) * lax.rsqrt(var + GN_EPS) * gn_g.astype(jnp.float32).reshape(h, n)
         + gn_b.astype(jnp.float32).reshape(h, n))
    bonus = jnp.sum(r_h * k_h * r_k.astype(jnp.float32), axis=-1, keepdims=True) * v_h
    return (y + bonus).reshape(bsz, seq, RWKV_DIM).astype(p.dtype) * g


def mla_mixer(p, q_norm_g, w_q_up, kv_norm_g, w_kv_up):
    bsz, seq, _ = p.shape
    h = MLA_HEADS
    q_lat, kv_lat, k_pe = jnp.split(p, [Q_LORA, Q_LORA + KV_LORA], axis=-1)
    q = (rms_norm(q_lat, q_norm_g) @ w_q_up).reshape(bsz, seq, h, QK_HEAD)
    q_nope, q_pe = jnp.split(q, [QK_NOPE], axis=-1)
    kv = (rms_norm(kv_lat, kv_norm_g) @ w_kv_up).reshape(bsz, seq, h, QK_NOPE + V_HEAD)
    k_nope, v = jnp.split(kv, [QK_NOPE], axis=-1)
    cos, sin = rope_tables(seq)
    q_pe = apply_rope(q_pe, cos[None, :, None, :], sin[None, :, None, :])
    k_pe = apply_rope(k_pe, cos[None], sin[None])
    scale = QK_HEAD ** -0.5
    n_blocks = seq // Q_BLOCK
    qn_b = q_nope.reshape(bsz, n_blocks, Q_BLOCK, h, QK_NOPE).swapaxes(0, 1)
    qp_b = q_pe.reshape(bsz, n_blocks, Q_BLOCK, h, QK_ROPE).swapaxes(0, 1)
    key_chunk = jnp.arange(seq) // CHUNK

    def block(args):
        qn, qp, blk = args
        s = (jnp.einsum('bqhd,bkhd->bhqk', qn, k_nope)
             + jnp.einsum('bqhr,bkr->bhqk', qp, k_pe)).astype(jnp.float32) * scale
        q_chunk = (blk * Q_BLOCK + jnp.arange(Q_BLOCK)) // CHUNK
        mask = key_chunk[None, :] <= q_chunk[:, None]
        s = jnp.where(mask[None, None], s, -jnp.inf)
        prob = jax.nn.softmax(s, axis=-1).astype(v.dtype)
        return jnp.einsum('bhqk,bkhd->bqhd', prob, v)

    o = lax.map(block, (qn_b, qp_b, jnp.arange(n_blocks)))
    return o.swapaxes(0, 1).reshape(bsz, seq, h * V_HEAD)


def setup_inputs(seed: int = 0) -> dict:
    key = jax.random.key(seed)
    ks = jax.random.split(key, 40)
    f32 = jnp.float32

    def nrm(i, shape, scale):
        return scale * jax.random.normal(ks[i], shape, f32)

    L = DEPTH
    w0_base = -6.0 + 5.0 * (jnp.arange(RWKV_DIM, dtype=f32) / (RWKV_DIM - 1)) ** 0.9
    return {
        "x": nrm(0, (BATCH, SEQ, D_MODEL), 1.0),
        "ffn1_w1": nrm(1, (L, D_MODEL, D_FF), D_MODEL ** -0.5),
        "ffn1_w3": nrm(2, (L, D_MODEL, D_FF), D_MODEL ** -0.5),
        "ffn1_w2": nrm(3, (L, D_FF, D_MODEL), BETA * D_FF ** -0.5),
        "ln1_g": 1.0 + nrm(4, (L, D_MODEL), 0.02),
        "ln1_b": nrm(5, (L, D_MODEL), 0.02),
        "w_in": nrm(6, (L, D_MODEL, IN_COLS), D_MODEL ** -0.5),
        "mu_shift": jax.random.uniform(ks[7], (L, SHIFT_COLS), f32),
        "w0": w0_base[None, :] + nrm(8, (L, RWKV_DIM), 0.1),
        "w_decay_up": nrm(9, (L, DECAY_LORA, RWKV_DIM), 0.1 * DECAY_LORA ** -0.5),
        "a0": nrm(10, (L, RWKV_DIM), 0.1),
        "w_iclr_up": nrm(11, (L, ICLR_LORA, RWKV_DIM), ICLR_LORA ** -0.5),
        "w_gate_up": nrm(12, (L, GATE_LORA, RWKV_DIM), GATE_LORA ** -0.5),
        "k_k": 0.85 + nrm(13, (L, RWKV_DIM), 0.05),
        "k_a": 1.0 + nrm(14, (L, RWKV_DIM), 0.05),
        "r_k": nrm(15, (L, RWKV_HEADS, HEAD_DIM_RWKV), 0.1),
        "gn_g": 1.0 + nrm(16, (L, RWKV_DIM), 0.02),
        "gn_b": nrm(17, (L, RWKV_DIM), 0.02),
        "q_norm_g": 1.0 + nrm(18, (L, Q_LORA), 0.02),
        "w_q_up": nrm(19, (L, Q_LORA, MLA_HEADS * QK_HEAD), Q_LORA ** -0.5),
        "kv_norm_g": 1.0 + nrm(20, (L, KV_LORA), 0.02),
        "w_kv_up": nrm(21, (L, KV_LORA, MLA_HEADS * (QK_NOPE + V_HEAD)), KV_LORA ** -0.5),
        "w_up_rwkv": nrm(22, (L, RWKV_DIM, D_MODEL), RWKV_DIM ** -0.5),
        "w_up_mla": nrm(23, (L, MLA_HEADS * V_HEAD, D_MODEL), (MLA_HEADS * V_HEAD) ** -0.5),
        "w_o": nrm(24, (L, D_MODEL, D_MODEL), BETA * D_MODEL ** -0.5),
        "ln2_g": 1.0 + nrm(25, (L, D_MODEL), 0.02),
        "ln2_b": nrm(26, (L, D_MODEL), 0.02),
        "ffn2_w1": nrm(27, (L, D_MODEL, D_FF), D_MODEL ** -0.5),
        "ffn2_w3": nrm(28, (L, D_MODEL, D_FF), D_MODEL ** -0.5),
        "ffn2_w2": nrm(29, (L, D_FF, D_MODEL), BETA * D_FF ** -0.5),
        "ln3_g": 1.0 + nrm(30, (L, D_MODEL), 0.02),
        "ln3_b": nrm(31, (L, D_MODEL), 0.02),
    }


def reference(x, ffn1_w1, ffn1_w3, ffn1_w2, ln1_g, ln1_b,
              w_in, mu_shift, w0, w_decay_up, a0, w_iclr_up, w_gate_up, k_k, k_a, r_k, gn_g, gn_b,
              q_norm_g, w_q_up, kv_norm_g, w_kv_up,
              w_up_rwkv, w_up_mla, w_o, ln2_g, ln2_b,
              ffn2_w1, ffn2_w3, ffn2_w2, ln3_g, ln3_b):
    h = x
    for l in range(DEPTH):
        h = layer_norm(ALPHA * h + 0.5 * swiglu(h, ffn1_w1[l], ffn1_w3[l], ffn1_w2[l]), ln1_g[l], ln1_b[l])
        proj = h @ w_in[l]
        p_shift, p_mla, gate_logits = jnp.split(proj, [SHIFT_COLS, SHIFT_COLS + MLA_COLS], axis=-1)
        y_rwkv = rwkv7_mixer(p_shift, mu_shift[l], w0[l], w_decay_up[l], a0[l], w_iclr_up[l],
                             w_gate_up[l], k_k[l], k_a[l], r_k[l], gn_g[l], gn_b[l]) @ w_up_rwkv[l]
        y_mla = mla_mixer(p_mla, q_norm_g[l], w_q_up[l], kv_norm_g[l], w_kv_up[l]) @ w_up_mla[l]
        g_rwkv, g_mla = jnp.split(jax.nn.sigmoid(gate_logits), 2, axis=-1)
        mix = (g_rwkv * y_rwkv + g_mla * y_mla) @ w_o[l]
        h = layer_norm(ALPHA * h + mix, ln2_g[l], ln2_b[l])
        h = layer_norm(ALPHA * h + 0.5 * swiglu(h, ffn2_w1[l], ffn2_w3[l], ffn2_w2[l]), ln3_g[l], ln3_b[l])
    return h
```

```python
import functools

import jax
import jax.numpy as jnp
from jax import lax
from jax.experimental import pallas as pl
from jax.experimental.pallas import tpu as pltpu

F32 = jnp.float32
BF16 = jnp.bfloat16

LANE = 128
SUBLANE = 8
VMEM_LIMIT_BYTES = 56 * 1024 * 1024

HEAD = 64
ROPE = 32
QK_HEAD = HEAD + ROPE
CHUNK = 64
DECAY_LORA = 64
ICLR_LORA = 64
GATE_LORA = 160
Q_LORA = 384
KV_LORA = 256
GN_EPS = 64e-5
LN_EPS = 1e-5
RMS_EPS = 1e-6
ROPE_THETA = 10000.0
LOG2E = 1.4426950408889634

RWKV_CHUNK = 128
RWKV_LEVELS = 7


def _dot(a, b):
    return jnp.dot(a, b, preferred_element_type=F32)


def _dot_nt(a, b):
    return lax.dot_general(a, b, (((1,), (1,)), ((), ())), preferred_element_type=F32)


def _dot_tn(a, b):
    return lax.dot_general(a, b, (((0,), (0,)), ((), ())), preferred_element_type=F32)


def _split2(x):
    hi = x.astype(BF16)
    lo = (x - hi.astype(F32)).astype(BF16)
    return hi, lo


def _split3(x):
    hi = x.astype(BF16)
    r1 = x - hi.astype(F32)
    mid = r1.astype(BF16)
    lo = (r1 - mid.astype(F32)).astype(BF16)
    return hi, mid, lo


def _layer_norm(z, g, b):
    mu = jnp.mean(z, axis=-1, keepdims=True)
    zc = z - mu
    var = jnp.mean(zc * zc, axis=-1, keepdims=True)
    return zc * lax.rsqrt(var + LN_EPS) * g + b


def _rms_norm(z, g):
    return z * lax.rsqrt(jnp.mean(z * z, axis=-1, keepdims=True) + RMS_EPS) * g


def _softplus(u):
    return jnp.maximum(u, 0.0) + jnp.log(1.0 + jnp.exp(-jnp.abs(u)))


def _const_spec(shape):
    nd = len(shape)
    return pl.BlockSpec(shape, lambda *_: (0,) * nd, pipeline_mode=pl.Buffered(1))


def _ffn_ln_kernel(x_ref, w1_ref, w3_ref, w2_ref, g_ref, b_ref, o_ref, *, alpha):
    x = x_ref[...]
    xb = x.astype(BF16)
    h1 = _dot(xb, w1_ref[...])
    h3 = _dot(xb, w3_ref[...])
    act = (h1 * jax.nn.sigmoid(h1) * h3).astype(BF16)
    y = _dot(act, w2_ref[...])
    o_ref[...] = _layer_norm(alpha * x + 0.5 * y, g_ref[...], b_ref[...])


def _ffn_ln(x2, w1, w3, w2, g, b, *, alpha, tm):
    t, d = x2.shape
    f = w1.shape[1]
    return pl.pallas_call(
        functools.partial(_ffn_ln_kernel, alpha=alpha),
        grid=(t // tm,),
        in_specs=[
            pl.BlockSpec((tm, d), lambda i: (i, 0)),
            _const_spec((d, f)), _const_spec((d, f)), _const_spec((f, d)),
            _const_spec((1, d)), _const_spec((1, d)),
        ],
        out_specs=pl.BlockSpec((tm, d), lambda i: (i, 0)),
        out_shape=jax.ShapeDtypeStruct((t, d), F32),
        compiler_params=pltpu.CompilerParams(
            dimension_semantics=("parallel",), vmem_limit_bytes=VMEM_LIMIT_BYTES),
        name="ffn_ln",
    )(x2, w1, w3, w2, g, b)


def _segsum64(x, ones2):
    outs = []
    for c in range(x.shape[1] // LANE):
        hi, lo = _split2(x[:, c * LANE:(c + 1) * LANE])
        outs.append(_dot(jnp.concatenate([hi, lo], axis=1), ones2))
    return outs[0] if len(outs) == 1 else jnp.concatenate(outs, axis=1)


def _rope(x, cos_t, sin_a, sin_b):
    return (x * cos_t + pltpu.roll(x, 16, axis=1) * sin_a
            + pltpu.roll(x, LANE - 16, axis=1) * sin_b)


def _proj_kernel(h_ref, cos_ref, sa_ref, sb_ref, win_ref, mu_ref, wda_ref, w0_ref, a0_ref,
                 wg_ref, kk_ref, ka_ref, ones2_ref, qg_ref, wq_ref, kvg_ref, wkv_ref, vone_ref,
                 r_o, ld_o, k_o, v_o, kk_o, a_o, g_o, q_o, kx_o, vx_o, carry_ref,
                 *, rd, n_shift, q_scale):
    s = pl.program_id(1)
    tm = h_ref.shape[1]
    hb = h_ref[0].astype(BF16)
    proj = _dot(hb, win_ref[...])

    ps = proj[:, :n_shift]
    prev = pltpu.roll(ps, 1, axis=0)
    first = jnp.where(s == 0, 0.0, carry_ref[SUBLANE - 1:SUBLANE, :])
    rowid = lax.broadcasted_iota(jnp.int32, (tm, 1), 0)
    prev = jnp.where(rowid == 0, first, prev)
    carry_ref[...] = ps[tm - SUBLANE:, :]
    ps = ps + mu_ref[...] * (prev - ps)

    r = ps[:, 0:rd]
    k = ps[:, rd:2 * rd]
    v = ps[:, 2 * rd:3 * rd]
    wa = ps[:, 3 * rd:3 * rd + LANE]
    gd = ps[:, 3 * rd + LANE:n_shift]

    lane = lax.broadcasted_iota(jnp.int32, (1, LANE), 1)
    t_in = jnp.where(lane < DECAY_LORA, jnp.tanh(wa), wa).astype(BF16)
    za = _dot(t_in, wda_ref[...])
    z = w0_ref[...] + za[:, :rd]
    w = -_softplus(-z) - 0.5
    ld_o[0] = -jnp.exp(w)
    a = jax.nn.sigmoid(a0_ref[...] + za[:, rd:])
    g_o[0] = _dot(jax.nn.sigmoid(gd).astype(BF16), wg_ref[...])

    kkr = k * kk_ref[...]
    ss = _segsum64(kkr * kkr, ones2_ref[...])
    kk_o[0] = kkr * lax.rsqrt(jnp.maximum(ss, 1e-24))
    k_o[0] = k * (1.0 + (a - 1.0) * ka_ref[...])
    r_o[0] = r
    v_o[0] = v
    a_o[0] = a

    pm = proj[:, n_shift:]
    q_lat = pm[:, :Q_LORA]
    kv_lat = pm[:, Q_LORA:Q_LORA + KV_LORA]
    kp = pm[:, Q_LORA + KV_LORA:]
    cos_t, sin_a, sin_b = cos_ref[...], sa_ref[...], sb_ref[...]
    q = _dot(_rms_norm(q_lat, qg_ref[...]).astype(BF16), wq_ref[...])
    kv = _dot(_rms_norm(kv_lat, kvg_ref[...]).astype(BF16), wkv_ref[...])
    kpr = _rope(kp, cos_t, sin_a, sin_b)
    nh = q.shape[1] // LANE
    for h in range(nh):
        sl = slice(h * LANE, (h + 1) * LANE)
        q_o[0, :, sl] = (_rope(q[:, sl], cos_t, sin_a, sin_b) * q_scale).astype(BF16)
        kx_o[0, :, sl] = (kv[:, sl] + kpr).astype(BF16)
    vx_o[0] = (kv[:, nh * LANE:] + vone_ref[...]).astype(BF16)


def _proj(h3, tabs, wts, *, tm):
    b, s, d = h3.shape
    (win, mu, wda, w0, a0, wg, kkw, kaw, ones2, qg, wq, kvg, wkv, vone) = wts
    rd = w0.shape[1]
    n_shift = mu.shape[1]
    hq = wq.shape[1]
    tok = lambda n: pl.BlockSpec((1, tm, n), lambda i, j: (i, j, 0))
    tab = pl.BlockSpec((tm, LANE), lambda i, j: (j, 0))
    f32o = jax.ShapeDtypeStruct((b, s, rd), F32)
    bfo = jax.ShapeDtypeStruct((b, s, hq), BF16)
    return pl.pallas_call(
        functools.partial(_proj_kernel, rd=rd, n_shift=n_shift,
                          q_scale=float(QK_HEAD ** -0.5 * LOG2E)),
        grid=(b, s // tm),
        in_specs=[tok(d), tab, tab, tab] + [_const_spec(w.shape) for w in wts],
        out_specs=[tok(rd)] * 7 + [tok(hq)] * 3,
        out_shape=[f32o] * 7 + [bfo] * 3,
        scratch_shapes=[pltpu.VMEM((SUBLANE, n_shift), F32)],
        compiler_params=pltpu.CompilerParams(
            dimension_semantics=("parallel", "arbitrary"), vmem_limit_bytes=VMEM_LIMIT_BYTES),
        name="proj",
    )(h3, *tabs, *wts)


def _wide_dot(a_w, b_w):
    c = a_w.shape[0]
    z = jnp.zeros((c, c), b_w.dtype)
    rhs = jnp.concatenate([jnp.concatenate([b_w[:, :c], z], axis=1),
                           jnp.concatenate([z, b_w[:, c:]], axis=1)], axis=0)
    return _dot(a_w, rhs)


def _rwkv_chunk(r, ld, k, v, kk, a, g, s_prev, cst):
    (tri, eye_w, lvl_ref, strict_w, incl_w, bd, ones2, m0, m1, rk, gng, gnb) = cst
    c = RWKV_CHUNK
    half = c // 2

    cl3 = _dot(tri, jnp.concatenate(_split3(ld), axis=1))
    cl = cl3[:, :LANE] + cl3[:, LANE:2 * LANE] + cl3[:, 2 * LANE:]
    ref = cl[half - 1:half, :]
    e = jnp.exp(cl - ref)
    einv = jnp.exp(ref - cl)
    rt = r * e
    at = -(kk * e) * jnp.exp(-ld)
    bt = kk * a * einv
    kt = k * einv

    at0, at1 = at * m0, at * m1
    lhs4 = jnp.concatenate([at0, at1, rt * m0, rt * m1], axis=0).astype(BF16)
    rhs2 = jnp.concatenate([bt, kt], axis=0).astype(BF16)
    gm = _dot_nt(lhs4, rhs2)
    pair = lambda r0, c0: jnp.concatenate(
        [gm[r0:r0 + c, c0:c0 + c], gm[r0 + c:r0 + 2 * c, c0:c0 + c]], axis=1)
    lab_w = jnp.where(strict_w, pair(0, 0), 0.0)
    lak_w = jnp.where(strict_w, pair(0, c), 0.0)
    arb_w = jnp.where(incl_w, pair(2 * c, 0), 0.0)
    ark_w = jnp.where(incl_w, pair(2 * c, c), 0.0)

    t_w = eye_w + lab_w * lvl_ref[0]
    for n in range(1, RWKV_LEVELS):
        lc = (lab_w * lvl_ref[n]).astype(BF16)
        x = _wide_dot(lc, t_w.astype(BF16))
        t_w = t_w + _wide_dot(t_w.astype(BF16), x.astype(BF16))

    v0, v1 = v * m0, v * m1
    v01 = jnp.concatenate([v0, v1], axis=0).astype(BF16)
    xv = _dot(lak_w.astype(BF16), v01)
    rhs = jnp.concatenate([jnp.concatenate([at0, xv * m0], axis=1),
                           jnp.concatenate([at1, xv * m1], axis=1)], axis=0).astype(BF16)
    wu = _dot(t_w.astype(BF16), rhs)
    sp = s_prev * jnp.exp(ref)
    ws = _dot_nt(jnp.concatenate([wu[:, :LANE], rt], axis=0).astype(BF16), sp.astype(BF16))
    u = ws[:c] + wu[:, LANE:]
    uv01 = jnp.concatenate([(u * m0).astype(BF16), (u * m1).astype(BF16), v01], axis=0)
    y = ws[c:] + _dot(jnp.concatenate([arb_w, ark_w], axis=1).astype(BF16), uv01)
    upd = _dot_tn(jnp.concatenate([u, v], axis=0).astype(BF16), rhs2)
    s_new = (sp + jnp.where(bd, upd, 0.0)) * e[c - 1:c, :]

    mu = _segsum64(y, ones2) * (1.0 / HEAD)
    yc = y - mu
    var = _segsum64(yc * yc, ones2) * (1.0 / HEAD)
    yn = yc * lax.rsqrt(var + GN_EPS) * gng + gnb
    bonus = _segsum64(r * k * rk, ones2) * v
    return ((yn + bonus) * g), s_new


def _rwkv_kernel(r_ref, ld_ref, k_ref, v_ref, kk_ref, a_ref, g_ref, rk_ref, gng_ref, gnb_ref,
                 tri_ref, lvl_ref, ones2_ref, o_ref, s_ref, *, group):
    c = RWKV_CHUNK

    @pl.when(pl.program_id(1) == 0)
    def _():
        s_ref[...] = jnp.zeros_like(s_ref)

    row = lax.broadcasted_iota(jnp.int32, (c, 2 * c), 0)
    colw = lax.broadcasted_iota(jnp.int32, (c, 2 * c), 1) & (c - 1)
    strict_w = row > colw
    incl_w = row >= colw
    eye_w = (row == colw).astype(F32)
    r2 = lax.broadcasted_iota(jnp.int32, (LANE, LANE), 0)
    c2 = lax.broadcasted_iota(jnp.int32, (LANE, LANE), 1)
    bd = (r2 >= HEAD) == (c2 >= HEAD)
    lane = lax.broadcasted_iota(jnp.int32, (1, LANE), 1)
    m0 = (lane < HEAD).astype(F32)
    m1 = 1.0 - m0
    cst = (tri_ref[...], eye_w, lvl_ref, strict_w, incl_w, bd, ones2_ref[...], m0, m1,
           rk_ref[...], gng_ref[...], gnb_ref[...])

    def body(i, carry):
        for j in range(group):
            b = i * group + j
            out, s_new = _rwkv_chunk(r_ref[b], ld_ref[b], k_ref[b], v_ref[b], kk_ref[b],
                                     a_ref[b], g_ref[b], s_ref[b], cst)
            s_ref[b] = s_new
            o_ref[b] = out.astype(o_ref.dtype)
        return carry

    lax.fori_loop(0, r_ref.shape[0] // group, body, 0)


def _rwkv(acts, rk, gng, gnb, consts, *, group):
    b, s, rd = acts[0].shape
    c = RWKV_CHUNK
    tri, lvl, ones2 = consts
    act = pl.BlockSpec((b, c, LANE), lambda p, t: (0, t, p))
    par = pl.BlockSpec((1, LANE), lambda p, t: (0, p))
    return pl.pallas_call(
        functools.partial(_rwkv_kernel, group=group),
        grid=(rd // LANE, s // c),
        in_specs=[act] * 7 + [par] * 3 + [_const_spec(tri.shape), _const_spec(lvl.shape),
                                          _const_spec(ones2.shape)],
        out_specs=act,
        out_shape=jax.ShapeDtypeStruct((b, s, rd), BF16),
        scratch_shapes=[pltpu.VMEM((b, LANE, LANE), F32)],
        compiler_params=pltpu.CompilerParams(
            dimension_semantics=("parallel", "arbitrary"), vmem_limit_bytes=VMEM_LIMIT_BYTES),
        name="rwkv",
    )(*acts, rk, gng, gnb, tri, lvl, ones2)


def _attn_kernel(q_ref, k_ref, v_ref, o_ref, *, tq):
    s_len = q_ref.shape[1]
    row = lax.broadcasted_iota(jnp.int32, (tq, tq), 0)
    col = lax.broadcasted_iota(jnp.int32, (tq, tq), 1)
    dmask = (col // CHUNK) <= (row // CHUNK)
    lane = lax.broadcasted_iota(jnp.int32, (1, LANE), 1)

    def q_body(i, carry):
        q0 = pl.multiple_of(i * tq, tq)
        qs = [q_ref[0, pl.ds(q0, tq), h * LANE:(h + 1) * LANE] for h in range(2)]

        def kv_step(j, st, masked):
            k0 = pl.multiple_of(j * tq, tq)
            new = []
            for h in range(2):
                m, acc = st[h]
                kh = k_ref[0, pl.ds(k0, tq), h * LANE:(h + 1) * LANE]
                sc = _dot_nt(qs[h], kh)
                if masked:
                    sc = jnp.where(dmask, sc, -jnp.inf)
                m_new = jnp.maximum(m, jnp.max(sc, axis=1, keepdims=True))
                p = jnp.exp2(sc - m_new)
                vh = v_ref[0, pl.ds(k0, tq), h * LANE:(h + 1) * LANE]
                acc = jnp.exp2(m - m_new) * acc + _dot(p.astype(BF16), vh)
                new.append((m_new, acc))
            return tuple(new)

        init = tuple((jnp.full((tq, 1), -jnp.inf, F32), jnp.zeros((tq, LANE), F32))
                     for _ in range(2))
        st = lax.fori_loop(0, i, lambda j, c: kv_step(j, c, False), init)
        (_, acc0), (_, acc1) = kv_step(i, st, True)
        out = jnp.where(lane < HEAD, acc0 / acc0[:, HEAD:HEAD + 1], acc1 / acc1[:, 0:1])
        o_ref[0, pl.ds(q0, tq), :] = out.astype(o_ref.dtype)
        return carry

    lax.fori_loop(0, s_len // tq, q_body, 0)


def _attn(q, k, v, *, tq):
    b, s, hq = q.shape
    pairs = hq // (2 * LANE)
    blk = pl.BlockSpec((1, s, 2 * LANE), lambda i, p: (i, 0, p))
    return pl.pallas_call(
        functools.partial(_attn_kernel, tq=tq),
        grid=(b, pairs),
        in_specs=[blk, blk, blk],
        out_specs=pl.BlockSpec((1, s, LANE), lambda i, p: (i, 0, p)),
        out_shape=jax.ShapeDtypeStruct((b, s, pairs * LANE), BF16),
        compiler_params=pltpu.CompilerParams(
            dimension_semantics=("parallel", "parallel"), vmem_limit_bytes=VMEM_LIMIT_BYTES),
        name="attn",
    )(q, k, v)


def _mix_ln_kernel(h_ref, yr_ref, om_ref, wgate_ref, wur_ref, wum_ref, wo_ref, g_ref, b_ref,
                   o_ref, *, alpha):
    h = h_ref[...]
    d = h.shape[1]
    gates = jax.nn.sigmoid(_dot(h.astype(BF16), wgate_ref[...]))
    y_r = _dot(yr_ref[...], wur_ref[...])
    y_m = _dot(om_ref[...], wum_ref[...])
    mixin = (gates[:, :d] * y_r + gates[:, d:] * y_m).astype(BF16)
    mix = _dot(mixin, wo_ref[...])
    o_ref[...] = _layer_norm(alpha * h + mix, g_ref[...], b_ref[...])


def _mix_ln(h2, yr, om, wgate, wur, wum, wo, g, b, *, alpha, tm):
    t, d = h2.shape
    rd = yr.shape[1]
    tok = lambda n: pl.BlockSpec((tm, n), lambda i: (i, 0))
    wts = (wgate, wur, wum, wo, g, b)
    return pl.pallas_call(
        functools.partial(_mix_ln_kernel, alpha=alpha),
        grid=(t // tm,),
        in_specs=[tok(d), tok(rd), tok(om.shape[1])] + [_const_spec(w.shape) for w in wts],
        out_specs=tok(d),
        out_shape=jax.ShapeDtypeStruct((t, d), F32),
        compiler_params=pltpu.CompilerParams(
            dimension_semantics=("parallel",), vmem_limit_bytes=VMEM_LIMIT_BYTES),
        name="mix_ln",
    )(h2, yr, om, *wts)


def _pad_cols(w, n):
    return jnp.pad(w, ((0, 0), (0, n - w.shape[1])))


def _rope_tables(s):
    inv_freq = ROPE_THETA ** (-jnp.arange(0, ROPE, 2, dtype=F32) / ROPE)
    ang = jnp.arange(s, dtype=F32)[:, None] * inv_freq[None, :]
    cos, sin = jnp.cos(ang), jnp.sin(ang)
    half = ROPE // 2
    z = lambda n: jnp.zeros((s, n), F32)
    cos_t = jnp.concatenate([jnp.ones((s, HEAD), F32), cos, cos, z(LANE - HEAD - ROPE)], axis=1)
    sin_a = jnp.concatenate([z(HEAD + half), sin, z(LANE - HEAD - ROPE)], axis=1)
    sin_b = jnp.concatenate([z(HEAD), -sin, z(LANE - HEAD - half)], axis=1)
    return cos_t, sin_a, sin_b


def _rwkv_consts():
    c = RWKV_CHUNK
    i = jnp.arange(c)[:, None]
    j = jnp.arange(c)[None, :]
    tri = (i >= j).astype(BF16)
    lv = []
    for lg in range(RWKV_LEVELS):
        m = ((i >> (lg + 1)) == (j >> (lg + 1))) & (((i >> lg) & 1) == 1) & (((j >> lg) & 1) == 0)
        lv.append(jnp.concatenate([m, m], axis=1).astype(F32))
    return tri, jnp.stack(lv)


def _ones2():
    i = jnp.arange(2 * LANE)[:, None] % LANE
    j = jnp.arange(LANE)[None, :]
    return ((i // HEAD) == (j // HEAD)).astype(BF16)


def kernel(x, ffn1_w1, ffn1_w3, ffn1_w2, ln1_g, ln1_b, w_in, mu_shift, w0, w_decay_up, a0, w_iclr_up, w_gate_up, k_k, k_a, r_k, gn_g, gn_b, q_norm_g, w_q_up, kv_norm_g, w_kv_up, w_up_rwkv, w_up_mla, w_o, ln2_g, ln2_b, ffn2_w1, ffn2_w3, ffn2_w2, ln3_g, ln3_b):
    b, s, d = x.shape
    depth = ffn1_w1.shape[0]
    rd = w0.shape[1]
    nh = d // LANE
    alpha = float((2.0 * depth) ** 0.25)
    f = ffn1_w1.shape[2]
    fp = -(-f // LANE) * LANE
    tm_ffn = min(256, b * s)
    tm_proj = min(256, s)
    tq = min(256, s)
    row = lambda p: p.reshape(1, -1).astype(F32)

    tabs = _rope_tables(s)
    tri, lvl = _rwkv_consts()
    ones2 = _ones2()
    lane = jnp.arange(nh * LANE) % LANE
    head = jnp.arange(nh * LANE) // LANE
    vone = jnp.where(head % 2 == 0, lane == HEAD, lane == 0).astype(F32)[None, :]

    n_rkv = 3 * rd
    n_lora = n_rkv + DECAY_LORA + ICLR_LORA
    n_shift = n_lora + GATE_LORA
    n_shift_p = n_lora + 2 * LANE
    n_mla = n_shift + Q_LORA + KV_LORA

    h = x.reshape(b * s, d)
    for l in range(depth):
        ffn = lambda hh, w1, w3, w2, g, bb: _ffn_ln(
            hh, _pad_cols(w1[l], fp).astype(BF16), _pad_cols(w3[l], fp).astype(BF16),
            jnp.pad(w2[l], ((0, fp - f), (0, 0))).astype(BF16), row(g[l]), row(bb[l]),
            alpha=alpha, tm=tm_ffn)
        h = ffn(h, ffn1_w1, ffn1_w3, ffn1_w2, ln1_g, ln1_b)

        w = w_in[l]
        z = lambda n: jnp.zeros((d, n), F32)
        kpe_blk = jnp.concatenate([z(HEAD), w[:, n_mla:n_mla + ROPE], z(LANE - HEAD - ROPE)], axis=1)
        win = jnp.concatenate([w[:, :n_shift], z(n_shift_p - n_shift), w[:, n_shift:n_mla], kpe_blk],
                              axis=1).astype(BF16)
        mu = jnp.pad(mu_shift[l], (0, n_shift_p - n_shift))[None, :]
        zl = jnp.zeros((DECAY_LORA, rd), F32)
        wda = jnp.concatenate([jnp.concatenate([w_decay_up[l], zl], axis=1),
                               jnp.concatenate([zl, w_iclr_up[l]], axis=1)], axis=0).astype(BF16)
        wg = jnp.pad(w_gate_up[l], ((0, 2 * LANE - GATE_LORA), (0, 0))).astype(BF16)
        wq = jnp.pad(w_q_up[l].reshape(Q_LORA, nh, QK_HEAD),
                     ((0, 0), (0, 0), (0, LANE - QK_HEAD))).reshape(Q_LORA, nh * LANE).astype(BF16)
        wkv3 = w_kv_up[l].reshape(KV_LORA, nh, 2 * HEAD)
        zk = jnp.zeros((KV_LORA, nh, HEAD), F32)
        wk = jnp.concatenate([wkv3[:, :, :HEAD], zk], axis=2).reshape(KV_LORA, nh * LANE)
        wv_even = jnp.concatenate([wkv3[:, :, HEAD:], zk], axis=2)
        wv_odd = jnp.concatenate([zk, wkv3[:, :, HEAD:]], axis=2)
        odd = (jnp.arange(nh) % 2 == 1)[None, :, None]
        wv = jnp.where(odd, wv_odd, wv_even).reshape(KV_LORA, nh * LANE)
        wkv = jnp.concatenate([wk, wv], axis=1).astype(BF16)
        wts = (win, mu, wda, row(w0[l]), row(a0[l]), wg, row(k_k[l]), row(k_a[l]), ones2,
               row(q_norm_g[l]), wq, row(kv_norm_g[l]), wkv, vone)
        r, ld, km, v, kk, a, g, q, kx, vx = _proj(h.reshape(b, s, d), tabs, wts, tm=tm_proj)

        yr = _rwkv((r, ld, km, v, kk, a, g), row(r_k[l]), row(gn_g[l]), row(gn_b[l]),
                   (tri, lvl, ones2), group=min(2, b))
        om = _attn(q, kx, vx, tq=tq)

        h = _mix_ln(h, yr.reshape(b * s, rd), om.reshape(b * s, -1),
                    w[:, n_mla + ROPE:].astype(BF16), w_up_rwkv[l].astype(BF16),
                    w_up_mla[l].astype(BF16), w_o[l].astype(BF16), row(ln2_g[l]), row(ln2_b[l]),
                    alpha=alpha, tm=tm_ffn)
        h = ffn(h, ffn2_w1, ffn2_w3, ffn2_w2, ln3_g, ln3_b)
    return h.reshape(b, s, d)
```

```python
import functools

import jax
import jax.numpy as jnp
from jax import lax
from jax.experimental import pallas as pl
from jax.experimental.pallas import tpu as pltpu

F32 = jnp.float32
BF16 = jnp.bfloat16

LANE = 128
SUBLANE = 8
VMEM_LIMIT_BYTES = 56 * 1024 * 1024

HEAD = 64
ROPE = 32
QK_HEAD = HEAD + ROPE
CHUNK = 64
DECAY_LORA = 64
ICLR_LORA = 64
GATE_LORA = 160
Q_LORA = 384
KV_LORA = 256
GN_EPS = 64e-5
LN_EPS = 1e-5
RMS_EPS = 1e-6
ROPE_THETA = 10000.0
LOG2E = 1.4426950408889634

RWKV_CHUNK = 128
RWKV_LEVELS = 7


def _dot(a, b):
    return jnp.dot(a, b, preferred_element_type=F32)


def _dot_nt(a, b):
    return lax.dot_general(a, b, (((1,), (1,)), ((), ())), preferred_element_type=F32)


def _dot_tn(a, b):
    return lax.dot_general(a, b, (((0,), (0,)), ((), ())), preferred_element_type=F32)


def _split2(x):
    hi = x.astype(BF16)
    lo = (x - hi.astype(F32)).astype(BF16)
    return hi, lo


def _split3(x):
    hi = x.astype(BF16)
    r1 = x - hi.astype(F32)
    mid = r1.astype(BF16)
    lo = (r1 - mid.astype(F32)).astype(BF16)
    return hi, mid, lo


def _layer_norm(z, g, b):
    mu = jnp.mean(z, axis=-1, keepdims=True)
    zc = z - mu
    var = jnp.mean(zc * zc, axis=-1, keepdims=True)
    return zc * lax.rsqrt(var + LN_EPS) * g + b


def _rms_norm(z, g):
    return z * lax.rsqrt(jnp.mean(z * z, axis=-1, keepdims=True) + RMS_EPS) * g


def _softplus(u):
    return jnp.maximum(u, 0.0) + jnp.log(1.0 + jnp.exp(-jnp.abs(u)))


def _const_spec(shape):
    nd = len(shape)
    return pl.BlockSpec(shape, lambda *_: (0,) * nd, pipeline_mode=pl.Buffered(1))


def _ffn_ln_kernel(x_ref, w1_ref, w3_ref, w2_ref, g_ref, b_ref, o_ref, *, alpha):
    x = x_ref[...]
    xb = x.astype(BF16)
    h1 = _dot(xb, w1_ref[...])
    h3 = _dot(xb, w3_ref[...])
    act = (h1 * jax.nn.sigmoid(h1) * h3).astype(BF16)
    y = _dot(act, w2_ref[...])
    o_ref[...] = _layer_norm(alpha * x + 0.5 * y, g_ref[...], b_ref[...])


def _ffn_ln(x2, w1, w3, w2, g, b, *, alpha, tm):
    t, d = x2.shape
    f = w1.shape[1]
    return pl.pallas_call(
        functools.partial(_ffn_ln_kernel, alpha=alpha),
        grid=(t // tm,),
        in_specs=[
            pl.BlockSpec((tm, d), lambda i: (i, 0)),
            _const_spec((d, f)), _const_spec((d, f)), _const_spec((f, d)),
            _const_spec((1, d)), _const_spec((1, d)),
        ],
        out_specs=pl.BlockSpec((tm, d), lambda i: (i, 0)),
        out_shape=jax.ShapeDtypeStruct((t, d), F32),
        compiler_params=pltpu.CompilerParams(
            dimension_semantics=("parallel",), vmem_limit_bytes=VMEM_LIMIT_BYTES),
        name="ffn_ln",
    )(x2, w1, w3, w2, g, b)


def _segsum64(x, ones2):
    outs = []
    for c in range(x.shape[1] // LANE):
        hi, lo = _split2(x[:, c * LANE:(c + 1) * LANE])
        outs.append(_dot(jnp.concatenate([hi, lo], axis=1), ones2))
    return outs[0] if len(outs) == 1 else jnp.concatenate(outs, axis=1)


def _rope(x, cos_t, sin_a, sin_b):
    return (x * cos_t + pltpu.roll(x, 16, axis=1) * sin_a
            + pltpu.roll(x, LANE - 16, axis=1) * sin_b)


def _proj_kernel(h_ref, cos_ref, sa_ref, sb_ref, win_ref, mu_ref, wda_ref, w0_ref, a0_ref,
                 wg_ref, kk_ref, ka_ref, ones2_ref, qg_ref, wq_ref, kvg_ref, wkv_ref, vone_ref,
                 r_o, ld_o, k_o, v_o, kk_o, a_o, g_o, q_o, kx_o, vx_o, carry_ref,
                 *, rd, n_shift, q_scale):
    s = pl.program_id(1)
    tm = h_ref.shape[1]
    hb = h_ref[0].astype(BF16)
    proj = _dot(hb, win_ref[...])

    ps = proj[:, :n_shift]
    prev = pltpu.roll(ps, 1, axis=0)
    first = jnp.where(s == 0, 0.0, carry_ref[SUBLANE - 1:SUBLANE, :])
    rowid = lax.broadcasted_iota(jnp.int32, (tm, 1), 0)
    prev = jnp.where(rowid == 0, first, prev)
    carry_ref[...] = ps[tm - SUBLANE:, :]
    ps = ps + mu_ref[...] * (prev - ps)

    r = ps[:, 0:rd]
    k = ps[:, rd:2 * rd]
    v = ps[:, 2 * rd:3 * rd]
    wa = ps[:, 3 * rd:3 * rd + LANE]
    gd = ps[:, 3 * rd + LANE:n_shift]

    lane = lax.broadcasted_iota(jnp.int32, (1, LANE), 1)
    t_in = jnp.where(lane < DECAY_LORA, jnp.tanh(wa), wa).astype(BF16)
    za = _dot(t_in, wda_ref[...])
    z = w0_ref[...] + za[:, :rd]
    w = -_softplus(-z) - 0.5
    ld_o[0] = -jnp.exp(w)
    a = jax.nn.sigmoid(a0_ref[...] + za[:, rd:])
    g_o[0] = _dot(jax.nn.sigmoid(gd).astype(BF16), wg_ref[...])

    kkr = k * kk_ref[...]
    ss = _segsum64(kkr * kkr, ones2_ref[...])
    kk_o[0] = kkr * lax.rsqrt(jnp.maximum(ss, 1e-24))
    k_o[0] = k * (1.0 + (a - 1.0) * ka_ref[...])
    r_o[0] = r
    v_o[0] = v
    a_o[0] = a

    pm = proj[:, n_shift:]
    q_lat = pm[:, :Q_LORA]
    kv_lat = pm[:, Q_LORA:Q_LORA + KV_LORA]
    kp = pm[:, Q_LORA + KV_LORA:]
    cos_t, sin_a, sin_b = cos_ref[...], sa_ref[...], sb_ref[...]
    q = _dot(_rms_norm(q_lat, qg_ref[...]).astype(BF16), wq_ref[...])
    kv = _dot(_rms_norm(kv_lat, kvg_ref[...]).astype(BF16), wkv_ref[...])
    kpr = _rope(kp, cos_t, sin_a, sin_b)
    nh = q.shape[1] // LANE
    for h in range(nh):
        sl = slice(h * LANE, (h + 1) * LANE)
        q_o[0, :, sl] = (_rope(q[:, sl], cos_t, sin_a, sin_b) * q_scale).astype(BF16)
        kx_o[0, :, sl] = (kv[:, sl] + kpr).astype(BF16)
    vx_o[0] = (kv[:, nh * LANE:] + vone_ref[...]).astype(BF16)


def _proj(h3, tabs, wts, *, tm):
    b, s, d = h3.shape
    (win, mu, wda, w0, a0, wg, kkw, kaw, ones2, qg, wq, kvg, wkv, vone) = wts
    rd = w0.shape[1]
    n_shift = mu.shape[1]
    hq = wq.shape[1]
    tok = lambda n: pl.BlockSpec((1, tm, n), lambda i, j: (i, j, 0))
    tab = pl.BlockSpec((tm, LANE), lambda i, j: (j, 0))
    f32o = jax.ShapeDtypeStruct((b, s, rd), F32)
    bfo = jax.ShapeDtypeStruct((b, s, hq), BF16)
    return pl.pallas_call(
        functools.partial(_proj_kernel, rd=rd, n_shift=n_shift,
                          q_scale=float(QK_HEAD ** -0.5 * LOG2E)),
        grid=(b, s // tm),
        in_specs=[tok(d), tab, tab, tab] + [_const_spec(w.shape) for w in wts],
        out_specs=[tok(rd)] * 7 + [tok(hq)] * 3,
        out_shape=[f32o] * 7 + [bfo] * 3,
        scratch_shapes=[pltpu.VMEM((SUBLANE, n_shift), F32)],
        compiler_params=pltpu.CompilerParams(
            dimension_semantics=("parallel", "arbitrary"), vmem_limit_bytes=VMEM_LIMIT_BYTES),
        name="proj",
    )(h3, *tabs, *wts)


def _wide_dot(a_w, b_w):
    c = a_w.shape[0]
    z = jnp.zeros((c, c), b_w.dtype)
    rhs = jnp.concatenate([jnp.concatenate([b_w[:, :c], z], axis=1),
                           jnp.concatenate([z, b_w[:, c:]], axis=1)], axis=0)
    return _dot(a_w, rhs)


def _rows(x, n):
    m = x.shape[0] // n
    return [x[i * m:(i + 1) * m] for i in range(n)]


def _segsum64_stacked(xs, ones2):
    parts = [jnp.concatenate(_split2(x), axis=1) for x in xs]
    return _rows(_dot(jnp.concatenate(parts, axis=0), ones2), len(xs))


def _rwkv_chunks(xs, s_prevs, cst):
    (tri, eye_w, lvl_ref, strict_w, incl_w, bd, ones2, m0, m1, rk, gng, gnb) = cst
    c = RWKV_CHUNK
    half = c // 2
    n = len(xs)
    each = lambda f, *ls: [f(*a) for a in zip(*ls)]
    rs, lds, ks, vs, kks, as_, gs = (list(t) for t in zip(*xs))

    ld3 = jnp.concatenate([jnp.concatenate(_split3(ld), axis=1) for ld in lds], axis=1)
    cl3 = _dot(tri, ld3)
    cls = [cl3[:, (3 * i) * LANE:(3 * i + 1) * LANE] + cl3[:, (3 * i + 1) * LANE:(3 * i + 2) * LANE]
           + cl3[:, (3 * i + 2) * LANE:(3 * i + 3) * LANE] for i in range(n)]
    refs = [cl[half - 1:half, :] for cl in cls]
    es = each(lambda cl, ref: jnp.exp(cl - ref), cls, refs)
    einvs = each(lambda cl, ref: jnp.exp(ref - cl), cls, refs)
    rts = each(lambda r, e: r * e, rs, es)
    ats = each(lambda kk, e, ld: -(kk * e) * jnp.exp(-ld), kks, es, lds)
    bts = each(lambda kk, a, ei: kk * a * ei, kks, as_, einvs)
    kts = each(lambda k, ei: k * ei, ks, einvs)

    lhs4s = each(lambda at, rt: jnp.concatenate([at * m0, at * m1, rt * m0, rt * m1],
                                                axis=0).astype(BF16), ats, rts)
    rhs2s = each(lambda bt, kt: jnp.concatenate([bt, kt], axis=0).astype(BF16), bts, kts)
    gms = each(_dot_nt, lhs4s, rhs2s)
    pair = lambda gm, r0, c0: jnp.concatenate(
        [gm[r0:r0 + c, c0:c0 + c], gm[r0 + c:r0 + 2 * c, c0:c0 + c]], axis=1)
    lab_ws = [jnp.where(strict_w, pair(gm, 0, 0), 0.0) for gm in gms]
    lak_ws = [jnp.where(strict_w, pair(gm, 0, c), 0.0) for gm in gms]
    arb_ws = [jnp.where(incl_w, pair(gm, 2 * c, 0), 0.0) for gm in gms]
    ark_ws = [jnp.where(incl_w, pair(gm, 2 * c, c), 0.0) for gm in gms]

    t_ws = [eye_w + lab * lvl_ref[0] for lab in lab_ws]
    for lv in range(1, RWKV_LEVELS):
        lcs = [(lab * lvl_ref[lv]).astype(BF16) for lab in lab_ws]
        tbs = [t.astype(BF16) for t in t_ws]
        xws = each(_wide_dot, lcs, tbs)
        tx = each(lambda tb, xw: _wide_dot(tb, xw.astype(BF16)), tbs, xws)
        t_ws = each(lambda t, d: t + d, t_ws, tx)

    v01s = [jnp.concatenate([v * m0, v * m1], axis=0).astype(BF16) for v in vs]
    xvs = each(lambda lak, v01: _dot(lak.astype(BF16), v01), lak_ws, v01s)
    rhss = each(lambda at, xv: jnp.concatenate(
        [jnp.concatenate([at * m0, xv * m0], axis=1),
         jnp.concatenate([at * m1, xv * m1], axis=1)], axis=0).astype(BF16), ats, xvs)
    wus = each(lambda t, rhs: _dot(t.astype(BF16), rhs), t_ws, rhss)
    sps = each(lambda s, ref: s * jnp.exp(ref), s_prevs, refs)
    wss = each(lambda wu, rt, sp: _dot_nt(
        jnp.concatenate([wu[:, :LANE], rt], axis=0).astype(BF16), sp.astype(BF16)), wus, rts, sps)
    us = each(lambda ws, wu: ws[:c] + wu[:, LANE:], wss, wus)
    ys = each(lambda ws, arb, ark, u, v01: ws[c:] + _dot(
        jnp.concatenate([arb, ark], axis=1).astype(BF16),
        jnp.concatenate([(u * m0).astype(BF16), (u * m1).astype(BF16), v01], axis=0)),
        wss, arb_ws, ark_ws, us, v01s)
    upds = each(lambda u, v, rhs2: _dot_tn(jnp.concatenate([u, v], axis=0).astype(BF16), rhs2),
                us, vs, rhs2s)
    s_news = each(lambda sp, upd, e: (sp + jnp.where(bd, upd, 0.0)) * e[c - 1:c, :],
                  sps, upds, es)

    mus = _segsum64_stacked(ys, ones2)
    ycs = each(lambda y, mu: y - mu * (1.0 / HEAD), ys, mus)
    vars_ = _segsum64_stacked([yc * yc for yc in ycs], ones2)
    bsums = _segsum64_stacked(each(lambda r, k: r * k * rk, rs, ks), ones2)
    outs = each(lambda yc, var, bs, v, g: (yc * lax.rsqrt(var * (1.0 / HEAD) + GN_EPS) * gng + gnb
                                           + bs * v) * g, ycs, vars_, bsums, vs, gs)
    return outs, s_news


def _rwkv_kernel(r_ref, ld_ref, k_ref, v_ref, kk_ref, a_ref, g_ref, rk_ref, gng_ref, gnb_ref,
                 tri_ref, lvl_ref, ones2_ref, o_ref, s_ref, *, group):
    c = RWKV_CHUNK

    @pl.when(pl.program_id(1) == 0)
    def _():
        s_ref[...] = jnp.zeros_like(s_ref)

    row = lax.broadcasted_iota(jnp.int32, (c, 2 * c), 0)
    colw = lax.broadcasted_iota(jnp.int32, (c, 2 * c), 1) & (c - 1)
    strict_w = row > colw
    incl_w = row >= colw
    eye_w = (row == colw).astype(F32)
    r2 = lax.broadcasted_iota(jnp.int32, (LANE, LANE), 0)
    c2 = lax.broadcasted_iota(jnp.int32, (LANE, LANE), 1)
    bd = (r2 >= HEAD) == (c2 >= HEAD)
    lane = lax.broadcasted_iota(jnp.int32, (1, LANE), 1)
    m0 = (lane < HEAD).astype(F32)
    m1 = 1.0 - m0
    cst = (tri_ref[...], eye_w, lvl_ref, strict_w, incl_w, bd, ones2_ref[...], m0, m1,
           rk_ref[...], gng_ref[...], gnb_ref[...])
    in_refs = (r_ref, ld_ref, k_ref, v_ref, kk_ref, a_ref, g_ref)

    def body(i, carry):
        bs = [i * group + j for j in range(group)]
        outs, s_news = _rwkv_chunks([tuple(ref[b] for ref in in_refs) for b in bs],
                                    [s_ref[b] for b in bs], cst)
        for b, out, s_new in zip(bs, outs, s_news):
            s_ref[b] = s_new
            o_ref[b] = out.astype(o_ref.dtype)
        return carry

    lax.fori_loop(0, r_ref.shape[0] // group, body, 0)


def _rwkv(acts, rk, gng, gnb, consts, *, group):
    b, s, rd = acts[0].shape
    c = RWKV_CHUNK
    tri, lvl, ones2 = consts
    act = pl.BlockSpec((b, c, LANE), lambda p, t: (0, t, p))
    par = pl.BlockSpec((1, LANE), lambda p, t: (0, p))
    return pl.pallas_call(
        functools.partial(_rwkv_kernel, group=group),
        grid=(rd // LANE, s // c),
        in_specs=[act] * 7 + [par] * 3 + [_const_spec(tri.shape), _const_spec(lvl.shape),
                                          _const_spec(ones2.shape)],
        out_specs=act,
        out_shape=jax.ShapeDtypeStruct((b, s, rd), BF16),
        scratch_shapes=[pltpu.VMEM((b, LANE, LANE), F32)],
        compiler_params=pltpu.CompilerParams(
            dimension_semantics=("parallel", "arbitrary"), vmem_limit_bytes=VMEM_LIMIT_BYTES),
        name="rwkv",
    )(*acts, rk, gng, gnb, tri, lvl, ones2)


def _attn_kernel(q_ref, k_ref, v_ref, bias_ref, o_ref, s_scr, p_scr, acc_scr, m_scr, alpha_scr,
                 *, tq, rb):
    nq = q_ref.shape[1] // tq
    n_steps = nq * (nq + 1) // 2
    lane = lax.broadcasted_iota(jnp.int32, (1, LANE), 1)
    heads = range(2)
    hs = lambda h: slice(h * LANE, (h + 1) * LANE)

    def advance(ij):
        i, j = ij
        last = j == i
        ni = jnp.where(last, jnp.minimum(i + 1, nq - 1), i)
        nj = jnp.where(last, jnp.where(i == nq - 1, j, 0), j + 1)
        return ni, nj

    def scores(ij, slot):
        q0 = pl.multiple_of(ij[0] * tq, tq)
        k0 = pl.multiple_of(ij[1] * tq, tq)
        bias = bias_ref[(ij[1] == ij[0]).astype(jnp.int32)]
        for h in heads:
            s_scr[slot, h] = bias + _dot_nt(q_ref[0, pl.ds(q0, tq), hs(h)],
                                            k_ref[0, pl.ds(k0, tq), hs(h)])

    def softmax(ij, slot):
        j = ij[1]
        cols = [pl.ds(c0, LANE) for c0 in range(0, tq, LANE)]
        for h in heads:
            for r0 in range(0, tq, rb):
                rows = pl.ds(r0, rb)
                mx = s_scr[slot, h, rows, cols[0]]
                for cs in cols[1:]:
                    mx = jnp.maximum(mx, s_scr[slot, h, rows, cs])
                mx = jnp.broadcast_to(jnp.max(mx, axis=1, keepdims=True), (rb, LANE))
                m_old = jnp.where(j == 0, -jnp.inf, m_scr[h, rows, :])
                m_new = jnp.maximum(m_old, mx)
                for cs in cols:
                    p_scr[slot, h, rows, cs] = jnp.exp2(s_scr[slot, h, rows, cs] - m_new).astype(BF16)
                m_scr[h, rows, :] = m_new
                alpha_scr[h, rows, :] = jnp.exp2(m_old - m_new)

    def values(ij, slot):
        k0 = pl.multiple_of(ij[1] * tq, tq)
        for h in heads:
            acc_scr[h] = (alpha_scr[h] * acc_scr[h]
                          + _dot(p_scr[slot, h], v_ref[0, pl.ds(k0, tq), hs(h)]))

    def finalize(i):
        q0 = pl.multiple_of(i * tq, tq)
        a0, a1 = acc_scr[0], acc_scr[1]
        out = jnp.where(lane < HEAD, a0 / a0[:, HEAD:HEAD + 1], a1 / a1[:, 0:1])
        o_ref[0, pl.ds(q0, tq), :] = out.astype(o_ref.dtype)

    acc_scr[...] = jnp.zeros_like(acc_scr)
    m_scr[...] = jnp.full_like(m_scr, -jnp.inf)
    zero = jnp.int32(0)
    ij0 = (zero, zero)
    ij1 = advance(ij0)
    ij2 = advance(ij1)
    scores(ij0, 0)
    softmax(ij0, 0)
    scores(ij1, 1)

    def step(carry, slot):
        ij_a, ij_b, ij_c = carry
        values(ij_a, slot)
        softmax(ij_b, 1 - slot)
        scores(ij_c, slot)

        @pl.when(ij_a[1] == ij_a[0])
        def _():
            finalize(ij_a[0])

        return ij_b, ij_c, advance(ij_c)

    carry = (ij0, ij1, ij2)
    first = n_steps % 2
    if first:
        carry = step(carry, 0)
    lax.fori_loop(0, n_steps // 2,
                  lambda _, c: step(step(c, first), 1 - first), carry)


def _attn(q, k, v, bias, *, tq):
    b, s, hq = q.shape
    pairs = hq // (2 * LANE)
    blk = pl.BlockSpec((1, s, 2 * LANE), lambda i, p: (i, 0, p))
    return pl.pallas_call(
        functools.partial(_attn_kernel, tq=tq, rb=min(64, tq)),
        grid=(b, pairs),
        in_specs=[blk, blk, blk, _const_spec(bias.shape)],
        out_specs=pl.BlockSpec((1, s, LANE), lambda i, p: (i, 0, p)),
        out_shape=jax.ShapeDtypeStruct((b, s, pairs * LANE), BF16),
        scratch_shapes=[pltpu.VMEM((2, 2, tq, tq), F32), pltpu.VMEM((2, 2, tq, tq), BF16),
                        pltpu.VMEM((2, tq, LANE), F32), pltpu.VMEM((2, tq, LANE), F32),
                        pltpu.VMEM((2, tq, LANE), F32)],
        compiler_params=pltpu.CompilerParams(
            dimension_semantics=("parallel", "parallel"), vmem_limit_bytes=VMEM_LIMIT_BYTES),
        name="attn",
    )(q, k, v, bias)


def _mix_ln_kernel(h_ref, yr_ref, om_ref, wgate_ref, wur_ref, wum_ref, wo_ref, g_ref, b_ref,
                   o_ref, *, alpha):
    h = h_ref[...]
    d = h.shape[1]
    gates = jax.nn.sigmoid(_dot(h.astype(BF16), wgate_ref[...]))
    y_r = _dot(yr_ref[...], wur_ref[...])
    y_m = _dot(om_ref[...], wum_ref[...])
    mixin = (gates[:, :d] * y_r + gates[:, d:] * y_m).astype(BF16)
    mix = _dot(mixin, wo_ref[...])
    o_ref[...] = _layer_norm(alpha * h + mix, g_ref[...], b_ref[...])


def _mix_ln(h2, yr, om, wgate, wur, wum, wo, g, b, *, alpha, tm):
    t, d = h2.shape
    rd = yr.shape[1]
    tok = lambda n: pl.BlockSpec((tm, n), lambda i: (i, 0))
    wts = (wgate, wur, wum, wo, g, b)
    return pl.pallas_call(
        functools.partial(_mix_ln_kernel, alpha=alpha),
        grid=(t // tm,),
        in_specs=[tok(d), tok(rd), tok(om.shape[1])] + [_const_spec(w.shape) for w in wts],
        out_specs=tok(d),
        out_shape=jax.ShapeDtypeStruct((t, d), F32),
        compiler_params=pltpu.CompilerParams(
            dimension_semantics=("parallel",), vmem_limit_bytes=VMEM_LIMIT_BYTES),
        name="mix_ln",
    )(h2, yr, om, *wts)


def _pad_cols(w, n):
    return jnp.pad(w, ((0, 0), (0, n - w.shape[1])))


def _rope_tables(s):
    inv_freq = ROPE_THETA ** (-jnp.arange(0, ROPE, 2, dtype=F32) / ROPE)
    ang = jnp.arange(s, dtype=F32)[:, None] * inv_freq[None, :]
    cos, sin = jnp.cos(ang), jnp.sin(ang)
    half = ROPE // 2
    z = lambda n: jnp.zeros((s, n), F32)
    cos_t = jnp.concatenate([jnp.ones((s, HEAD), F32), cos, cos, z(LANE - HEAD - ROPE)], axis=1)
    sin_a = jnp.concatenate([z(HEAD + half), sin, z(LANE - HEAD - ROPE)], axis=1)
    sin_b = jnp.concatenate([z(HEAD), -sin, z(LANE - HEAD - half)], axis=1)
    return cos_t, sin_a, sin_b


def _rwkv_consts():
    c = RWKV_CHUNK
    i = jnp.arange(c)[:, None]
    j = jnp.arange(c)[None, :]
    tri = (i >= j).astype(BF16)
    lv = []
    for lg in range(RWKV_LEVELS):
        m = ((i >> (lg + 1)) == (j >> (lg + 1))) & (((i >> lg) & 1) == 1) & (((j >> lg) & 1) == 0)
        lv.append(jnp.concatenate([m, m], axis=1).astype(F32))
    return tri, jnp.stack(lv)


def _ones2():
    i = jnp.arange(2 * LANE)[:, None] % LANE
    j = jnp.arange(LANE)[None, :]
    return ((i // HEAD) == (j // HEAD)).astype(BF16)


def _attn_bias(tq):
    i = jnp.arange(tq)[:, None] // CHUNK
    j = jnp.arange(tq)[None, :] // CHUNK
    stair = jnp.where(j <= i, 0.0, -jnp.inf).astype(F32)
    return jnp.stack([jnp.zeros((tq, tq), F32), stair])


def kernel(x, ffn1_w1, ffn1_w3, ffn1_w2, ln1_g, ln1_b, w_in, mu_shift, w0, w_decay_up, a0, w_iclr_up, w_gate_up, k_k, k_a, r_k, gn_g, gn_b, q_norm_g, w_q_up, kv_norm_g, w_kv_up, w_up_rwkv, w_up_mla, w_o, ln2_g, ln2_b, ffn2_w1, ffn2_w3, ffn2_w2, ln3_g, ln3_b):
    b, s, d = x.shape
    depth = ffn1_w1.shape[0]
    rd = w0.shape[1]
    nh = d // LANE
    alpha = float((2.0 * depth) ** 0.25)
    f = ffn1_w1.shape[2]
    fp = -(-f // LANE) * LANE
    tm_ffn = min(256, b * s)
    tm_proj = min(256, s)
    tq = min(512, s)
    row = lambda p: p.reshape(1, -1).astype(F32)

    tabs = _rope_tables(s)
    tri, lvl = _rwkv_consts()
    ones2 = _ones2()
    bias = _attn_bias(tq)
    lane = jnp.arange(nh * LANE) % LANE
    head = jnp.arange(nh * LANE) // LANE
    vone = jnp.where(head % 2 == 0, lane == HEAD, lane == 0).astype(F32)[None, :]

    n_rkv = 3 * rd
    n_lora = n_rkv + DECAY_LORA + ICLR_LORA
    n_shift = n_lora + GATE_LORA
    n_shift_p = n_lora + 2 * LANE
    n_mla = n_shift + Q_LORA + KV_LORA

    h = x.reshape(b * s, d)
    for l in range(depth):
        ffn = lambda hh, w1, w3, w2, g, bb: _ffn_ln(
            hh, _pad_cols(w1[l], fp).astype(BF16), _pad_cols(w3[l], fp).astype(BF16),
            jnp.pad(w2[l], ((0, fp - f), (0, 0))).astype(BF16), row(g[l]), row(bb[l]),
            alpha=alpha, tm=tm_ffn)
        h = ffn(h, ffn1_w1, ffn1_w3, ffn1_w2, ln1_g, ln1_b)

        w = w_in[l]
        z = lambda n: jnp.zeros((d, n), F32)
        kpe_blk = jnp.concatenate([z(HEAD), w[:, n_mla:n_mla + ROPE], z(LANE - HEAD - ROPE)], axis=1)
        win = jnp.concatenate([w[:, :n_shift], z(n_shift_p - n_shift), w[:, n_shift:n_mla], kpe_blk],
                              axis=1).astype(BF16)
        mu = jnp.pad(mu_shift[l], (0, n_shift_p - n_shift))[None, :]
        zl = jnp.zeros((DECAY_LORA, rd), F32)
        wda = jnp.concatenate([jnp.concatenate([w_decay_up[l], zl], axis=1),
                               jnp.concatenate([zl, w_iclr_up[l]], axis=1)], axis=0).astype(BF16)
        wg = jnp.pad(w_gate_up[l], ((0, 2 * LANE - GATE_LORA), (0, 0))).astype(BF16)
        wq = jnp.pad(w_q_up[l].reshape(Q_LORA, nh, QK_HEAD),
                     ((0, 0), (0, 0), (0, LANE - QK_HEAD))).reshape(Q_LORA, nh * LANE).astype(BF16)
        wkv3 = w_kv_up[l].reshape(KV_LORA, nh, 2 * HEAD)
        zk = jnp.zeros((KV_LORA, nh, HEAD), F32)
        wk = jnp.concatenate([wkv3[:, :, :HEAD], zk], axis=2).reshape(KV_LORA, nh * LANE)
        wv_even = jnp.concatenate([wkv3[:, :, HEAD:], zk], axis=2)
        wv_odd = jnp.concatenate([zk, wkv3[:, :, HEAD:]], axis=2)
        odd = (jnp.arange(nh) % 2 == 1)[None, :, None]
        wv = jnp.where(odd, wv_odd, wv_even).reshape(KV_LORA, nh * LANE)
        wkv = jnp.concatenate([wk, wv], axis=1).astype(BF16)
        wts = (win, mu, wda, row(w0[l]), row(a0[l]), wg, row(k_k[l]), row(k_a[l]), ones2,
               row(q_norm_g[l]), wq, row(kv_norm_g[l]), wkv, vone)
        r, ld, km, v, kk, a, g, q, kx, vx = _proj(h.reshape(b, s, d), tabs, wts, tm=tm_proj)

        yr = _rwkv((r, ld, km, v, kk, a, g), row(r_k[l]), row(gn_g[l]), row(gn_b[l]),
                   (tri, lvl, ones2), group=min(8, b))
        om = _attn(q, kx, vx, bias, tq=tq)

        h = _mix_ln(h, yr.reshape(b * s, rd), om.reshape(b * s, -1),
                    w[:, n_mla + ROPE:].astype(BF16), w_up_rwkv[l].astype(BF16),
                    w_up_mla[l].astype(BF16), w_o[l].astype(BF16), row(ln2_g[l]), row(ln2_b[l]),
                    alpha=alpha, tm=tm_ffn)
        h = ffn(h, ffn2_w1, ffn2_w3, ffn2_w2, ln3_g, ln3_b)
    return h.reshape(b, s, d)
```

```python
import functools

import jax
import jax.numpy as jnp
from jax import lax
from jax.experimental import pallas as pl
from jax.experimental.pallas import tpu as pltpu

F32 = jnp.float32
BF16 = jnp.bfloat16

LANE = 128
SUBLANE = 8
VMEM_LIMIT_BYTES = 56 * 1024 * 1024

HEAD = 64
ROPE = 32
QK_HEAD = HEAD + ROPE
CHUNK = 64
DECAY_LORA = 64
ICLR_LORA = 64
GATE_LORA = 160
Q_LORA = 384
KV_LORA = 256
GN_EPS = 64e-5
LN_EPS = 1e-5
RMS_EPS = 1e-6
ROPE_THETA = 10000.0
LOG2E = 1.4426950408889634

RWKV_CHUNK = 128
RWKV_LEVELS = 7


def _dot(a, b):
    return jnp.dot(a, b, preferred_element_type=F32)


def _dot_nt(a, b):
    return lax.dot_general(a, b, (((1,), (1,)), ((), ())), preferred_element_type=F32)


def _dot_tn(a, b):
    return lax.dot_general(a, b, (((0,), (0,)), ((), ())), preferred_element_type=F32)


def _split2(x):
    hi = x.astype(BF16)
    lo = (x - hi.astype(F32)).astype(BF16)
    return hi, lo


def _split3(x):
    hi = x.astype(BF16)
    r1 = x - hi.astype(F32)
    mid = r1.astype(BF16)
    lo = (r1 - mid.astype(F32)).astype(BF16)
    return hi, mid, lo


def _layer_norm(z, g, b):
    mu = jnp.mean(z, axis=-1, keepdims=True)
    zc = z - mu
    var = jnp.mean(zc * zc, axis=-1, keepdims=True)
    return zc * lax.rsqrt(var + LN_EPS) * g + b


def _rms_norm(z, g):
    return z * lax.rsqrt(jnp.mean(z * z, axis=-1, keepdims=True) + RMS_EPS) * g


def _softplus(u):
    return jnp.maximum(u, 0.0) + jnp.log(1.0 + jnp.exp(-jnp.abs(u)))


def _const_spec(shape):
    nd = len(shape)
    return pl.BlockSpec(shape, lambda *_: (0,) * nd, pipeline_mode=pl.Buffered(1))


def _swiglu_ln(x, w1_ref, w3_ref, w2_ref, g_ref, b_ref, alpha):
    xb = x.astype(BF16)
    h1 = _dot(xb, w1_ref[...])
    h3 = _dot(xb, w3_ref[...])
    act = (h1 * jax.nn.sigmoid(h1) * h3).astype(BF16)
    y = _dot(act, w2_ref[...])
    return _layer_norm(alpha * x + 0.5 * y, g_ref[...], b_ref[...])


def _ffn_ln_kernel(x_ref, w1_ref, w3_ref, w2_ref, g_ref, b_ref, o_ref, *, alpha):
    o_ref[...] = _swiglu_ln(x_ref[...], w1_ref, w3_ref, w2_ref, g_ref, b_ref, alpha)


def _ffn_ln(x2, w1, w3, w2, g, b, *, alpha, tm):
    t, d = x2.shape
    f = w1.shape[1]
    return pl.pallas_call(
        functools.partial(_ffn_ln_kernel, alpha=alpha),
        grid=(t // tm,),
        in_specs=[
            pl.BlockSpec((tm, d), lambda i: (i, 0)),
            _const_spec((d, f)), _const_spec((d, f)), _const_spec((f, d)),
            _const_spec((1, d)), _const_spec((1, d)),
        ],
        out_specs=pl.BlockSpec((tm, d), lambda i: (i, 0)),
        out_shape=jax.ShapeDtypeStruct((t, d), F32),
        compiler_params=pltpu.CompilerParams(
            dimension_semantics=("parallel",), vmem_limit_bytes=VMEM_LIMIT_BYTES),
        name="ffn_ln",
    )(x2, w1, w3, w2, g, b)


def _segsum64(x, ones2):
    outs = []
    for c in range(x.shape[1] // LANE):
        hi, lo = _split2(x[:, c * LANE:(c + 1) * LANE])
        outs.append(_dot(jnp.concatenate([hi, lo], axis=1), ones2))
    return outs[0] if len(outs) == 1 else jnp.concatenate(outs, axis=1)


def _rope(x, cos_t, sin_a, sin_b):
    return (x * cos_t + pltpu.roll(x, 16, axis=1) * sin_a
            + pltpu.roll(x, LANE - 16, axis=1) * sin_b)


def _proj_kernel(h_ref, cos_ref, sa_ref, sb_ref, win_ref, mu_ref, wda_ref, w0_ref, a0_ref,
                 wg_ref, kk_ref, ka_ref, ones2_ref, qg_ref, wq_ref, kvg_ref, wkv_ref, vone_ref,
                 r_o, ld_o, k_o, v_o, kk_o, a_o, g_o, q_o, kx_o, vx_o, carry_ref,
                 *, rd, n_shift, q_scale):
    s = pl.program_id(1)
    tm = h_ref.shape[1]
    hb = h_ref[0].astype(BF16)
    proj = _dot(hb, win_ref[...])

    ps = proj[:, :n_shift]
    prev = pltpu.roll(ps, 1, axis=0)
    first = jnp.where(s == 0, 0.0, carry_ref[SUBLANE - 1:SUBLANE, :])
    rowid = lax.broadcasted_iota(jnp.int32, (tm, 1), 0)
    prev = jnp.where(rowid == 0, first, prev)
    carry_ref[...] = ps[tm - SUBLANE:, :]
    ps = ps + mu_ref[...] * (prev - ps)

    r = ps[:, 0:rd]
    k = ps[:, rd:2 * rd]
    v = ps[:, 2 * rd:3 * rd]
    wa = ps[:, 3 * rd:3 * rd + LANE]
    gd = ps[:, 3 * rd + LANE:n_shift]

    lane = lax.broadcasted_iota(jnp.int32, (1, LANE), 1)
    t_in = jnp.where(lane < DECAY_LORA, jnp.tanh(wa), wa).astype(BF16)
    za = _dot(t_in, wda_ref[...])
    z = w0_ref[...] + za[:, :rd]
    w = -_softplus(-z) - 0.5
    ld_o[0] = -jnp.exp(w)
    a = jax.nn.sigmoid(a0_ref[...] + za[:, rd:])
    g_o[0] = _dot(jax.nn.sigmoid(gd).astype(BF16), wg_ref[...])

    kkr = k * kk_ref[...]
    ss = _segsum64(kkr * kkr, ones2_ref[...])
    kk_o[0] = kkr * lax.rsqrt(jnp.maximum(ss, 1e-24))
    k_o[0] = k * (1.0 + (a - 1.0) * ka_ref[...])
    r_o[0] = r
    v_o[0] = v
    a_o[0] = a

    pm = proj[:, n_shift:]
    q_lat = pm[:, :Q_LORA]
    kv_lat = pm[:, Q_LORA:Q_LORA + KV_LORA]
    kp = pm[:, Q_LORA + KV_LORA:]
    cos_t, sin_a, sin_b = cos_ref[...], sa_ref[...], sb_ref[...]
    q = _dot(_rms_norm(q_lat, qg_ref[...]).astype(BF16), wq_ref[...])
    kv = _dot(_rms_norm(kv_lat, kvg_ref[...]).astype(BF16), wkv_ref[...])
    kpr = _rope(kp, cos_t, sin_a, sin_b)
    nh = q.shape[1] // LANE
    for h in range(nh):
        sl = slice(h * LANE, (h + 1) * LANE)
        q_o[0, :, sl] = (_rope(q[:, sl], cos_t, sin_a, sin_b) * q_scale).astype(BF16)
        kx_o[0, :, sl] = (kv[:, sl] + kpr).astype(BF16)
    vx_o[0] = (kv[:, nh * LANE:] + vone_ref[...]).astype(BF16)


def _proj(h3, tabs, wts, *, tm):
    b, s, d = h3.shape
    (win, mu, wda, w0, a0, wg, kkw, kaw, ones2, qg, wq, kvg, wkv, vone) = wts
    rd = w0.shape[1]
    n_shift = mu.shape[1]
    hq = wq.shape[1]
    tok = lambda n: pl.BlockSpec((1, tm, n), lambda i, j: (i, j, 0))
    tab = pl.BlockSpec((tm, LANE), lambda i, j: (j, 0))
    f32o = jax.ShapeDtypeStruct((b, s, rd), F32)
    bfo = jax.ShapeDtypeStruct((b, s, hq), BF16)
    return pl.pallas_call(
        functools.partial(_proj_kernel, rd=rd, n_shift=n_shift,
                          q_scale=float(QK_HEAD ** -0.5 * LOG2E)),
        grid=(b, s // tm),
        in_specs=[tok(d), tab, tab, tab] + [_const_spec(w.shape) for w in wts],
        out_specs=[tok(rd)] * 7 + [tok(hq)] * 3,
        out_shape=[f32o] * 7 + [bfo] * 3,
        scratch_shapes=[pltpu.VMEM((SUBLANE, n_shift), F32)],
        compiler_params=pltpu.CompilerParams(
            dimension_semantics=("parallel", "arbitrary"), vmem_limit_bytes=VMEM_LIMIT_BYTES),
        name="proj",
    )(h3, *tabs, *wts)


def _wide_dot(a_w, b_w):
    c = a_w.shape[0]
    z = jnp.zeros((c, c), b_w.dtype)
    rhs = jnp.concatenate([jnp.concatenate([b_w[:, :c], z], axis=1),
                           jnp.concatenate([z, b_w[:, c:]], axis=1)], axis=0)
    return _dot(a_w, rhs)


def _rows(x, n):
    m = x.shape[0] // n
    return [x[i * m:(i + 1) * m] for i in range(n)]


def _segsum64_stacked(xs, ones2):
    parts = [jnp.concatenate(_split2(x), axis=1) for x in xs]
    return _rows(_dot(jnp.concatenate(parts, axis=0), ones2), len(xs))


def _rwkv_chunks(xs, s_prevs, cst):
    (tri, eye_w, lvl_ref, strict_w, incl_w, bd, ones2, m0, m1, rk, gng, gnb) = cst
    c = RWKV_CHUNK
    half = c // 2
    n = len(xs)
    each = lambda f, *ls: [f(*a) for a in zip(*ls)]
    rs, lds, ks, vs, kks, as_, gs = (list(t) for t in zip(*xs))

    ld3 = jnp.concatenate([jnp.concatenate(_split3(ld), axis=1) for ld in lds], axis=1)
    cl3 = _dot(tri, ld3)
    cls = [cl3[:, (3 * i) * LANE:(3 * i + 1) * LANE] + cl3[:, (3 * i + 1) * LANE:(3 * i + 2) * LANE]
           + cl3[:, (3 * i + 2) * LANE:(3 * i + 3) * LANE] for i in range(n)]
    refs = [cl[half - 1:half, :] for cl in cls]
    es = each(lambda cl, ref: jnp.exp(cl - ref), cls, refs)
    einvs = each(lambda cl, ref: jnp.exp(ref - cl), cls, refs)
    rts = each(lambda r, e: r * e, rs, es)
    ats = each(lambda kk, e, ld: -(kk * e) * jnp.exp(-ld), kks, es, lds)
    bts = each(lambda kk, a, ei: kk * a * ei, kks, as_, einvs)
    kts = each(lambda k, ei: k * ei, ks, einvs)

    lhs4s = each(lambda at, rt: jnp.concatenate([at * m0, at * m1, rt * m0, rt * m1],
                                                axis=0).astype(BF16), ats, rts)
    rhs2s = each(lambda bt, kt: jnp.concatenate([bt, kt], axis=0).astype(BF16), bts, kts)
    gms = each(_dot_nt, lhs4s, rhs2s)
    pair = lambda gm, r0, c0: jnp.concatenate(
        [gm[r0:r0 + c, c0:c0 + c], gm[r0 + c:r0 + 2 * c, c0:c0 + c]], axis=1)
    lab_ws = [jnp.where(strict_w, pair(gm, 0, 0), 0.0) for gm in gms]
    lak_ws = [jnp.where(strict_w, pair(gm, 0, c), 0.0) for gm in gms]
    arb_ws = [jnp.where(incl_w, pair(gm, 2 * c, 0), 0.0) for gm in gms]
    ark_ws = [jnp.where(incl_w, pair(gm, 2 * c, c), 0.0) for gm in gms]

    t_ws = [eye_w + lab * lvl_ref[0] for lab in lab_ws]
    for lv in range(1, RWKV_LEVELS):
        lcs = [(lab * lvl_ref[lv]).astype(BF16) for lab in lab_ws]
        tbs = [t.astype(BF16) for t in t_ws]
        xws = each(_wide_dot, lcs, tbs)
        tx = each(lambda tb, xw: _wide_dot(tb, xw.astype(BF16)), tbs, xws)
        t_ws = each(lambda t, d: t + d, t_ws, tx)

    v01s = [jnp.concatenate([v * m0, v * m1], axis=0).astype(BF16) for v in vs]
    xvs = each(lambda lak, v01: _dot(lak.astype(BF16), v01), lak_ws, v01s)
    rhss = each(lambda at, xv: jnp.concatenate(
        [jnp.concatenate([at * m0, xv * m0], axis=1),
         jnp.concatenate([at * m1, xv * m1], axis=1)], axis=0).astype(BF16), ats, xvs)
    wus = each(lambda t, rhs: _dot(t.astype(BF16), rhs), t_ws, rhss)
    sps = each(lambda s, ref: s * jnp.exp(ref), s_prevs, refs)
    wss = each(lambda wu, rt, sp: _dot_nt(
        jnp.concatenate([wu[:, :LANE], rt], axis=0).astype(BF16), sp.astype(BF16)), wus, rts, sps)
    us = each(lambda ws, wu: ws[:c] + wu[:, LANE:], wss, wus)
    ys = each(lambda ws, arb, ark, u, v01: ws[c:] + _dot(
        jnp.concatenate([arb, ark], axis=1).astype(BF16),
        jnp.concatenate([(u * m0).astype(BF16), (u * m1).astype(BF16), v01], axis=0)),
        wss, arb_ws, ark_ws, us, v01s)
    upds = each(lambda u, v, rhs2: _dot_tn(jnp.concatenate([u, v], axis=0).astype(BF16), rhs2),
                us, vs, rhs2s)
    s_news = each(lambda sp, upd, e: (sp + jnp.where(bd, upd, 0.0)) * e[c - 1:c, :],
                  sps, upds, es)

    mus = _segsum64_stacked(ys, ones2)
    ycs = each(lambda y, mu: y - mu * (1.0 / HEAD), ys, mus)
    vars_ = _segsum64_stacked([yc * yc for yc in ycs], ones2)
    bsums = _segsum64_stacked(each(lambda r, k: r * k * rk, rs, ks), ones2)
    outs = each(lambda yc, var, bs, v, g: (yc * lax.rsqrt(var * (1.0 / HEAD) + GN_EPS) * gng + gnb
                                           + bs * v) * g, ycs, vars_, bsums, vs, gs)
    return outs, s_news


def _rwkv_kernel(r_ref, ld_ref, k_ref, v_ref, kk_ref, a_ref, g_ref, rk_ref, gng_ref, gnb_ref,
                 tri_ref, lvl_ref, ones2_ref, o_ref, s_ref, *, group):
    c = RWKV_CHUNK

    @pl.when(pl.program_id(1) == 0)
    def _():
        s_ref[...] = jnp.zeros_like(s_ref)

    row = lax.broadcasted_iota(jnp.int32, (c, 2 * c), 0)
    colw = lax.broadcasted_iota(jnp.int32, (c, 2 * c), 1) & (c - 1)
    strict_w = row > colw
    incl_w = row >= colw
    eye_w = (row == colw).astype(F32)
    r2 = lax.broadcasted_iota(jnp.int32, (LANE, LANE), 0)
    c2 = lax.broadcasted_iota(jnp.int32, (LANE, LANE), 1)
    bd = (r2 >= HEAD) == (c2 >= HEAD)
    lane = lax.broadcasted_iota(jnp.int32, (1, LANE), 1)
    m0 = (lane < HEAD).astype(F32)
    m1 = 1.0 - m0
    cst = (tri_ref[...], eye_w, lvl_ref, strict_w, incl_w, bd, ones2_ref[...], m0, m1,
           rk_ref[...], gng_ref[...], gnb_ref[...])
    in_refs = (r_ref, ld_ref, k_ref, v_ref, kk_ref, a_ref, g_ref)

    def body(i, carry):
        bs = [i * group + j for j in range(group)]
        outs, s_news = _rwkv_chunks([tuple(ref[b] for ref in in_refs) for b in bs],
                                    [s_ref[b] for b in bs], cst)
        for b, out, s_new in zip(bs, outs, s_news):
            s_ref[b] = s_new
            o_ref[b] = out.astype(o_ref.dtype)
        return carry

    lax.fori_loop(0, r_ref.shape[0] // group, body, 0)


def _rwkv(acts, rk, gng, gnb, consts, *, group):
    b, s, rd = acts[0].shape
    c = RWKV_CHUNK
    tri, lvl, ones2 = consts
    act = pl.BlockSpec((b, c, LANE), lambda p, t: (0, t, p))
    par = pl.BlockSpec((1, LANE), lambda p, t: (0, p))
    return pl.pallas_call(
        functools.partial(_rwkv_kernel, group=group),
        grid=(rd // LANE, s // c),
        in_specs=[act] * 7 + [par] * 3 + [_const_spec(tri.shape), _const_spec(lvl.shape),
                                          _const_spec(ones2.shape)],
        out_specs=act,
        out_shape=jax.ShapeDtypeStruct((b, s, rd), BF16),
        scratch_shapes=[pltpu.VMEM((b, LANE, LANE), F32)],
        compiler_params=pltpu.CompilerParams(
            dimension_semantics=("parallel", "arbitrary"), vmem_limit_bytes=VMEM_LIMIT_BYTES),
        name="rwkv",
    )(*acts, rk, gng, gnb, tri, lvl, ones2)


def _attn_kernel(q_ref, k_ref, v_ref, bias_ref, o_ref, s_scr, p_scr, acc_scr, m_scr, alpha_scr,
                 *, tq, tk, rb):
    nq = q_ref.shape[1] // tq
    ratio = tk // tq
    n_steps = sum(i // ratio + 1 for i in range(nq))
    lane = lax.broadcasted_iota(jnp.int32, (1, LANE), 1)
    heads = range(2)
    hs = lambda h: slice(h * LANE, (h + 1) * LANE)

    def advance(ij):
        i, j = ij
        last = j == i // ratio
        ni = jnp.where(last, jnp.minimum(i + 1, nq - 1), i)
        nj = jnp.where(last, jnp.where(i == nq - 1, j, 0), j + 1)
        return ni, nj

    def scores(ij, slot, heads=heads):
        i, j = ij
        q0 = pl.multiple_of(i * tq, tq)
        k0 = pl.multiple_of(j * tk, tk)
        bias = bias_ref[jnp.where(j == i // ratio, 1 + i % ratio, 0)]
        for h in heads:
            s_scr[slot, h] = bias + _dot_nt(q_ref[0, pl.ds(q0, tq), hs(h)],
                                            k_ref[0, pl.ds(k0, tk), hs(h)])

    def softmax(ij, slot, heads=heads):
        j = ij[1]
        cols = [pl.ds(c0, LANE) for c0 in range(0, tk, LANE)]
        for h in heads:
            for r0 in range(0, tq, rb):
                rows = pl.ds(r0, rb)
                mx = s_scr[slot, h, rows, cols[0]]
                for cs in cols[1:]:
                    mx = jnp.maximum(mx, s_scr[slot, h, rows, cs])
                mx = jnp.broadcast_to(jnp.max(mx, axis=1, keepdims=True), (rb, LANE))
                m_old = jnp.where(j == 0, -jnp.inf, m_scr[h, rows, :])
                m_new = jnp.maximum(m_old, mx)
                for cs in cols:
                    p_scr[h, rows, cs] = jnp.exp2(
                        (s_scr[slot, h, rows, cs] - m_new).astype(BF16))
                m_scr[h, rows, :] = m_new
                alpha_scr[h, rows, :] = jnp.exp2(m_old - m_new)

    def values(ij, heads=heads):
        k0 = pl.multiple_of(ij[1] * tk, tk)
        for h in heads:
            acc_scr[h] = (alpha_scr[h] * acc_scr[h]
                          + _dot(p_scr[h], v_ref[0, pl.ds(k0, tk), hs(h)]))

    def finalize(i):
        q0 = pl.multiple_of(i * tq, tq)
        a0, a1 = acc_scr[0], acc_scr[1]
        out = jnp.where(lane < HEAD, a0 / a0[:, HEAD:HEAD + 1], a1 / a1[:, 0:1])
        o_ref[0, pl.ds(q0, tq), :] = out.astype(o_ref.dtype)

    acc_scr[...] = jnp.zeros_like(acc_scr)
    m_scr[...] = jnp.full_like(m_scr, -jnp.inf)
    zero = jnp.int32(0)
    ij0 = (zero, zero)
    scores(ij0, 0)

    def step(carry, slot):
        ij_a, ij_b = carry
        for h in heads:
            scores(ij_b, 1 - slot, (h,))
            softmax(ij_a, slot, (h,))
            values(ij_a, (h,))

        @pl.when(ij_a[1] == ij_a[0] // ratio)
        def _():
            finalize(ij_a[0])

        return ij_b, advance(ij_b)

    carry = (ij0, advance(ij0))
    first = n_steps % 2
    if first:
        carry = step(carry, 0)
    lax.fori_loop(0, n_steps // 2,
                  lambda _, c: step(step(c, first), 1 - first), carry)


def _attn(q, k, v, bias, *, tq, tk):
    b, s, hq = q.shape
    pairs = hq // (2 * LANE)
    blk = pl.BlockSpec((1, s, 2 * LANE), lambda i, p: (i, 0, p))
    return pl.pallas_call(
        functools.partial(_attn_kernel, tq=tq, tk=tk, rb=min(64, tq)),
        grid=(b, pairs),
        in_specs=[blk, blk, blk, _const_spec(bias.shape)],
        out_specs=pl.BlockSpec((1, s, LANE), lambda i, p: (i, 0, p)),
        out_shape=jax.ShapeDtypeStruct((b, s, pairs * LANE), BF16),
        scratch_shapes=[pltpu.VMEM((2, 2, tq, tk), F32), pltpu.VMEM((2, tq, tk), BF16),
                        pltpu.VMEM((2, tq, LANE), F32), pltpu.VMEM((2, tq, LANE), F32),
                        pltpu.VMEM((2, tq, LANE), F32)],
        compiler_params=pltpu.CompilerParams(
            dimension_semantics=("parallel", "parallel"), vmem_limit_bytes=VMEM_LIMIT_BYTES),
        name="attn",
    )(q, k, v, bias)


def _mix_ffn_ln_kernel(h_ref, yr_ref, om_ref, wgate_ref, wur_ref, wum_ref, wo_ref, g2_ref, b2_ref,
                       w1_ref, w3_ref, w2_ref, g3_ref, b3_ref, o_ref, *, alpha):
    h = h_ref[...]
    d = h.shape[1]
    gates = jax.nn.sigmoid(_dot(h.astype(BF16), wgate_ref[...]))
    y_r = _dot(yr_ref[...], wur_ref[...])
    y_m = _dot(om_ref[...], wum_ref[...])
    mixin = (gates[:, :d] * y_r + gates[:, d:] * y_m).astype(BF16)
    mix = _dot(mixin, wo_ref[...])
    h2 = _layer_norm(alpha * h + mix, g2_ref[...], b2_ref[...])
    o_ref[...] = _swiglu_ln(h2, w1_ref, w3_ref, w2_ref, g3_ref, b3_ref, alpha)


def _mix_ffn_ln(h2, yr, om, wts, *, alpha, tm):
    t, d = h2.shape
    tok = lambda n: pl.BlockSpec((tm, n), lambda i: (i, 0))
    return pl.pallas_call(
        functools.partial(_mix_ffn_ln_kernel, alpha=alpha),
        grid=(t // tm,),
        in_specs=[tok(d), tok(yr.shape[1]), tok(om.shape[1])] + [_const_spec(w.shape) for w in wts],
        out_specs=tok(d),
        out_shape=jax.ShapeDtypeStruct((t, d), F32),
        compiler_params=pltpu.CompilerParams(
            dimension_semantics=("parallel",), vmem_limit_bytes=VMEM_LIMIT_BYTES),
        name="mix_ffn_ln",
    )(h2, yr, om, *wts)


def _pad_cols(w, n):
    return jnp.pad(w, ((0, 0), (0, n - w.shape[1])))


def _rope_tables(s):
    inv_freq = ROPE_THETA ** (-jnp.arange(0, ROPE, 2, dtype=F32) / ROPE)
    ang = jnp.arange(s, dtype=F32)[:, None] * inv_freq[None, :]
    cos, sin = jnp.cos(ang), jnp.sin(ang)
    half = ROPE // 2
    z = lambda n: jnp.zeros((s, n), F32)
    cos_t = jnp.concatenate([jnp.ones((s, HEAD), F32), cos, cos, z(LANE - HEAD - ROPE)], axis=1)
    sin_a = jnp.concatenate([z(HEAD + half), sin, z(LANE - HEAD - ROPE)], axis=1)
    sin_b = jnp.concatenate([z(HEAD), -sin, z(LANE - HEAD - half)], axis=1)
    return cos_t, sin_a, sin_b


def _rwkv_consts():
    c = RWKV_CHUNK
    i = jnp.arange(c)[:, None]
    j = jnp.arange(c)[None, :]
    tri = (i >= j).astype(BF16)
    lv = []
    for lg in range(RWKV_LEVELS):
        m = ((i >> (lg + 1)) == (j >> (lg + 1))) & (((i >> lg) & 1) == 1) & (((j >> lg) & 1) == 0)
        lv.append(jnp.concatenate([m, m], axis=1).astype(F32))
    return tri, jnp.stack(lv)


def _ones2():
    i = jnp.arange(2 * LANE)[:, None] % LANE
    j = jnp.arange(LANE)[None, :]
    return ((i // HEAD) == (j // HEAD)).astype(BF16)


def _attn_bias(tq, tk):
    j = jnp.arange(tk)[None, :] // CHUNK
    tiles = [jnp.zeros((tq, tk), F32)]
    for v in range(tk // tq):
        i = (v * tq + jnp.arange(tq)[:, None]) // CHUNK
        tiles.append(jnp.where(j <= i, 0.0, -jnp.inf).astype(F32))
    return jnp.stack(tiles)


def kernel(x, ffn1_w1, ffn1_w3, ffn1_w2, ln1_g, ln1_b, w_in, mu_shift, w0, w_decay_up, a0, w_iclr_up, w_gate_up, k_k, k_a, r_k, gn_g, gn_b, q_norm_g, w_q_up, kv_norm_g, w_kv_up, w_up_rwkv, w_up_mla, w_o, ln2_g, ln2_b, ffn2_w1, ffn2_w3, ffn2_w2, ln3_g, ln3_b):
    b, s, d = x.shape
    depth = ffn1_w1.shape[0]
    rd = w0.shape[1]
    nh = d // LANE
    alpha = float((2.0 * depth) ** 0.25)
    f = ffn1_w1.shape[2]
    fp = -(-f // LANE) * LANE
    tm_ffn = min(256, b * s)
    tm_proj = min(512, s)
    tq = min(512, s)
    tk = min(2 * tq, s)
    row = lambda p: p.reshape(1, -1).astype(F32)

    tabs = _rope_tables(s)
    tri, lvl = _rwkv_consts()
    ones2 = _ones2()
    bias = _attn_bias(tq, tk)
    lane = jnp.arange(nh * LANE) % LANE
    head = jnp.arange(nh * LANE) // LANE
    vone = jnp.where(head % 2 == 0, lane == HEAD, lane == 0).astype(F32)[None, :]

    n_rkv = 3 * rd
    n_lora = n_rkv + DECAY_LORA + ICLR_LORA
    n_shift = n_lora + GATE_LORA
    n_shift_p = n_lora + 2 * LANE
    n_mla = n_shift + Q_LORA + KV_LORA

    h = x.reshape(b * s, d)
    for l in range(depth):
        ffn_wts = lambda w1, w3, w2, g, bb: (
            _pad_cols(w1[l], fp).astype(BF16), _pad_cols(w3[l], fp).astype(BF16),
            jnp.pad(w2[l], ((0, fp - f), (0, 0))).astype(BF16), row(g[l]), row(bb[l]))
        h = _ffn_ln(h, *ffn_wts(ffn1_w1, ffn1_w3, ffn1_w2, ln1_g, ln1_b), alpha=alpha, tm=tm_ffn)

        w = w_in[l]
        z = lambda n: jnp.zeros((d, n), F32)
        kpe_blk = jnp.concatenate([z(HEAD), w[:, n_mla:n_mla + ROPE], z(LANE - HEAD - ROPE)], axis=1)
        win = jnp.concatenate([w[:, :n_shift], z(n_shift_p - n_shift), w[:, n_shift:n_mla], kpe_blk],
                              axis=1).astype(BF16)
        mu = jnp.pad(mu_shift[l], (0, n_shift_p - n_shift))[None, :]
        zl = jnp.zeros((DECAY_LORA, rd), F32)
        wda = jnp.concatenate([jnp.concatenate([w_decay_up[l], zl], axis=1),
                               jnp.concatenate([zl, w_iclr_up[l]], axis=1)], axis=0).astype(BF16)
        wg = jnp.pad(w_gate_up[l], ((0, 2 * LANE - GATE_LORA), (0, 0))).astype(BF16)
        wq = jnp.pad(w_q_up[l].reshape(Q_LORA, nh, QK_HEAD),
                     ((0, 0), (0, 0), (0, LANE - QK_HEAD))).reshape(Q_LORA, nh * LANE).astype(BF16)
        wkv3 = w_kv_up[l].reshape(KV_LORA, nh, 2 * HEAD)
        zk = jnp.zeros((KV_LORA, nh, HEAD), F32)
        wk = jnp.concatenate([wkv3[:, :, :HEAD], zk], axis=2).reshape(KV_LORA, nh * LANE)
        wv_even = jnp.concatenate([wkv3[:, :, HEAD:], zk], axis=2)
        wv_odd = jnp.concatenate([zk, wkv3[:, :, HEAD:]], axis=2)
        odd = (jnp.arange(nh) % 2 == 1)[None, :, None]
        wv = jnp.where(odd, wv_odd, wv_even).reshape(KV_LORA, nh * LANE)
        wkv = jnp.concatenate([wk, wv], axis=1).astype(BF16)
        wts = (win, mu, wda, row(w0[l]), row(a0[l]), wg, row(k_k[l]), row(k_a[l]), ones2,
               row(q_norm_g[l]), wq, row(kv_norm_g[l]), wkv, vone)
        r, ld, km, v, kk, a, g, q, kx, vx = _proj(h.reshape(b, s, d), tabs, wts, tm=tm_proj)

        yr = _rwkv((r, ld, km, v, kk, a, g), row(r_k[l]), row(gn_g[l]), row(gn_b[l]),
                   (tri, lvl, ones2), group=min(8, b))
        om = _attn(q, kx, vx, bias, tq=tq, tk=tk)

        mix_wts = (w[:, n_mla + ROPE:].astype(BF16), w_up_rwkv[l].astype(BF16),
                   w_up_mla[l].astype(BF16), w_o[l].astype(BF16), row(ln2_g[l]), row(ln2_b[l]))
        h = _mix_ffn_ln(h, yr.reshape(b * s, rd), om.reshape(b * s, -1),
                        mix_wts + ffn_wts(ffn2_w1, ffn2_w3, ffn2_w2, ln3_g, ln3_b),
                        alpha=alpha, tm=tm_ffn)
    return h.reshape(b, s, d)
```

```python
import functools

import jax
import jax.numpy as jnp
from jax import lax
from jax.experimental import pallas as pl
from jax.experimental.pallas import tpu as pltpu

F32 = jnp.float32
BF16 = jnp.bfloat16

LANE = 128
SUBLANE = 8
VMEM_LIMIT_BYTES = 56 * 1024 * 1024

HEAD = 64
ROPE = 32
QK_HEAD = HEAD + ROPE
CHUNK = 64
DECAY_LORA = 64
ICLR_LORA = 64
GATE_LORA = 160
Q_LORA = 384
KV_LORA = 256
GN_EPS = 64e-5
LN_EPS = 1e-5
RMS_EPS = 1e-6
ROPE_THETA = 10000.0
LOG2E = 1.4426950408889634

RWKV_CHUNK = 128
RWKV_LEVELS = 7


def _dot(a, b):
    return jnp.dot(a, b, preferred_element_type=F32)


def _dot_nt(a, b):
    return lax.dot_general(a, b, (((1,), (1,)), ((), ())), preferred_element_type=F32)


def _dot_tn(a, b):
    return lax.dot_general(a, b, (((0,), (0,)), ((), ())), preferred_element_type=F32)


def _split2(x):
    hi = x.astype(BF16)
    lo = (x - hi.astype(F32)).astype(BF16)
    return hi, lo


def _split3(x):
    hi = x.astype(BF16)
    r1 = x - hi.astype(F32)
    mid = r1.astype(BF16)
    lo = (r1 - mid.astype(F32)).astype(BF16)
    return hi, mid, lo


def _layer_norm(z, g, b):
    mu = jnp.mean(z, axis=-1, keepdims=True)
    zc = z - mu
    var = jnp.mean(zc * zc, axis=-1, keepdims=True)
    return zc * lax.rsqrt(var + LN_EPS) * g + b


def _rms_norm(z, g):
    return z * lax.rsqrt(jnp.mean(z * z, axis=-1, keepdims=True) + RMS_EPS) * g


def _softplus(u):
    return jnp.maximum(u, 0.0) + jnp.log(1.0 + jnp.exp(-jnp.abs(u)))


def _const_spec(shape):
    nd = len(shape)
    return pl.BlockSpec(shape, lambda *_: (0,) * nd, pipeline_mode=pl.Buffered(1))


def _swiglu_ln(x, w1_ref, w3_ref, w2_ref, g_ref, b_ref, alpha):
    xb = x.astype(BF16)
    h1 = _dot(xb, w1_ref[...])
    h3 = _dot(xb, w3_ref[...])
    act = (h1 * jax.nn.sigmoid(h1) * h3).astype(BF16)
    y = _dot(act, w2_ref[...])
    return _layer_norm(alpha * x + 0.5 * y, g_ref[...], b_ref[...])


def _ffn_ln_kernel(x_ref, w1_ref, w3_ref, w2_ref, g_ref, b_ref, o_ref, *, alpha):
    o_ref[...] = _swiglu_ln(x_ref[...], w1_ref, w3_ref, w2_ref, g_ref, b_ref, alpha)


def _ffn_ln(x2, w1, w3, w2, g, b, *, alpha, tm):
    t, d = x2.shape
    f = w1.shape[1]
    return pl.pallas_call(
        functools.partial(_ffn_ln_kernel, alpha=alpha),
        grid=(t // tm,),
        in_specs=[
            pl.BlockSpec((tm, d), lambda i: (i, 0)),
            _const_spec((d, f)), _const_spec((d, f)), _const_spec((f, d)),
            _const_spec((1, d)), _const_spec((1, d)),
        ],
        out_specs=pl.BlockSpec((tm, d), lambda i: (i, 0)),
        out_shape=jax.ShapeDtypeStruct((t, d), F32),
        compiler_params=pltpu.CompilerParams(
            dimension_semantics=("parallel",), vmem_limit_bytes=VMEM_LIMIT_BYTES),
        name="ffn_ln",
    )(x2, w1, w3, w2, g, b)


def _segsum64(x, ones2):
    outs = []
    for c in range(x.shape[1] // LANE):
        hi, lo = _split2(x[:, c * LANE:(c + 1) * LANE])
        outs.append(_dot(jnp.concatenate([hi, lo], axis=1), ones2))
    return outs[0] if len(outs) == 1 else jnp.concatenate(outs, axis=1)


def _rope(x, cos_t, sin_a, sin_b):
    return (x * cos_t + pltpu.roll(x, 16, axis=1) * sin_a
            + pltpu.roll(x, LANE - 16, axis=1) * sin_b)


def _proj_kernel(h_ref, cos_ref, sa_ref, sb_ref, win_ref, mu_ref, wda_ref, w0_ref, a0_ref,
                 wg_ref, kk_ref, ka_ref, ones2_ref, qg_ref, wq_ref, kvg_ref, wkv_ref, vone_ref,
                 r_o, ld_o, k_o, v_o, kk_o, a_o, g_o, q_o, kx_o, vx_o, carry_ref,
                 *, rd, n_shift, q_scale):
    s = pl.program_id(1)
    tm = h_ref.shape[1]
    hb = h_ref[0].astype(BF16)
    proj = _dot(hb, win_ref[...])

    ps = proj[:, :n_shift]
    prev = pltpu.roll(ps, 1, axis=0)
    first = jnp.where(s == 0, 0.0, carry_ref[SUBLANE - 1:SUBLANE, :])
    rowid = lax.broadcasted_iota(jnp.int32, (tm, 1), 0)
    prev = jnp.where(rowid == 0, first, prev)
    carry_ref[...] = ps[tm - SUBLANE:, :]
    ps = ps + mu_ref[...] * (prev - ps)

    r = ps[:, 0:rd]
    k = ps[:, rd:2 * rd]
    v = ps[:, 2 * rd:3 * rd]
    wa = ps[:, 3 * rd:3 * rd + LANE]
    gd = ps[:, 3 * rd + LANE:n_shift]

    lane = lax.broadcasted_iota(jnp.int32, (1, LANE), 1)
    t_in = jnp.where(lane < DECAY_LORA, jnp.tanh(wa), wa).astype(BF16)
    za = _dot(t_in, wda_ref[...])
    z = w0_ref[...] + za[:, :rd]
    w = -_softplus(-z) - 0.5
    ld_o[0] = -jnp.exp(w)
    a = jax.nn.sigmoid(a0_ref[...] + za[:, rd:])
    g_o[0] = _dot(jax.nn.sigmoid(gd).astype(BF16), wg_ref[...])

    kkr = k * kk_ref[...]
    ss = _segsum64(kkr * kkr, ones2_ref[...])
    kk_o[0] = kkr * lax.rsqrt(jnp.maximum(ss, 1e-24))
    k_o[0] = k * (1.0 + (a - 1.0) * ka_ref[...])
    r_o[0] = r
    v_o[0] = v
    a_o[0] = a

    pm = proj[:, n_shift:]
    q_lat = pm[:, :Q_LORA]
    kv_lat = pm[:, Q_LORA:Q_LORA + KV_LORA]
    kp = pm[:, Q_LORA + KV_LORA:]
    cos_t, sin_a, sin_b = cos_ref[...], sa_ref[...], sb_ref[...]
    q = _dot(_rms_norm(q_lat, qg_ref[...]).astype(BF16), wq_ref[...])
    kv = _dot(_rms_norm(kv_lat, kvg_ref[...]).astype(BF16), wkv_ref[...])
    kpr = _rope(kp, cos_t, sin_a, sin_b)
    nh = q.shape[1] // LANE
    for h in range(nh):
        sl = slice(h * LANE, (h + 1) * LANE)
        q_o[0, :, sl] = (_rope(q[:, sl], cos_t, sin_a, sin_b) * q_scale).astype(BF16)
        kx_o[0, :, sl] = (kv[:, sl] + kpr).astype(BF16)
    vx_o[0] = (kv[:, nh * LANE:] + vone_ref[...]).astype(BF16)


def _proj(h3, tabs, wts, *, tm):
    b, s, d = h3.shape
    (win, mu, wda, w0, a0, wg, kkw, kaw, ones2, qg, wq, kvg, wkv, vone) = wts
    rd = w0.shape[1]
    n_shift = mu.shape[1]
    hq = wq.shape[1]
    tok = lambda n: pl.BlockSpec((1, tm, n), lambda i, j: (i, j, 0))
    tab = pl.BlockSpec((tm, LANE), lambda i, j: (j, 0))
    f32o = jax.ShapeDtypeStruct((b, s, rd), F32)
    bfo = jax.ShapeDtypeStruct((b, s, hq), BF16)
    return pl.pallas_call(
        functools.partial(_proj_kernel, rd=rd, n_shift=n_shift,
                          q_scale=float(QK_HEAD ** -0.5 * LOG2E)),
        grid=(b, s // tm),
        in_specs=[tok(d), tab, tab, tab] + [_const_spec(w.shape) for w in wts],
        out_specs=[tok(rd)] * 7 + [tok(hq)] * 3,
        out_shape=[f32o] * 7 + [bfo] * 3,
        scratch_shapes=[pltpu.VMEM((SUBLANE, n_shift), F32)],
        compiler_params=pltpu.CompilerParams(
            dimension_semantics=("parallel", "arbitrary"), vmem_limit_bytes=VMEM_LIMIT_BYTES),
        name="proj",
    )(h3, *tabs, *wts)


def _wide_dot(a_w, b_w):
    c = a_w.shape[0]
    z = jnp.zeros((c, c), b_w.dtype)
    rhs = jnp.concatenate([jnp.concatenate([b_w[:, :c], z], axis=1),
                           jnp.concatenate([z, b_w[:, c:]], axis=1)], axis=0)
    return _dot(a_w, rhs)


def _rows(x, n):
    m = x.shape[0] // n
    return [x[i * m:(i + 1) * m] for i in range(n)]


def _segsum64_stacked(xs, ones2):
    parts = [jnp.concatenate(_split2(x), axis=1) for x in xs]
    return _rows(_dot(jnp.concatenate(parts, axis=0), ones2), len(xs))


def _rwkv_chunks(xs, s_prevs, cst):
    (tri, eye_w, lvl_ref, strict_w, incl_w, bd, ones2, m0, m1, rk, gng, gnb) = cst
    c = RWKV_CHUNK
    half = c // 2
    n = len(xs)
    each = lambda f, *ls: [f(*a) for a in zip(*ls)]
    rs, lds, ks, vs, kks, as_, gs = (list(t) for t in zip(*xs))

    ld3 = jnp.concatenate([jnp.concatenate(_split3(ld), axis=1) for ld in lds], axis=1)
    cl3 = _dot(tri, ld3)
    cls = [cl3[:, (3 * i) * LANE:(3 * i + 1) * LANE] + cl3[:, (3 * i + 1) * LANE:(3 * i + 2) * LANE]
           + cl3[:, (3 * i + 2) * LANE:(3 * i + 3) * LANE] for i in range(n)]
    refs = [cl[half - 1:half, :] for cl in cls]
    es = each(lambda cl, ref: jnp.exp(cl - ref), cls, refs)
    einvs = each(lambda cl, ref: jnp.exp(ref - cl), cls, refs)
    rts = each(lambda r, e: r * e, rs, es)
    ats = each(lambda kk, e, ld: -(kk * e) * jnp.exp(-ld), kks, es, lds)
    bts = each(lambda kk, a, ei: kk * a * ei, kks, as_, einvs)
    kts = each(lambda k, ei: k * ei, ks, einvs)

    lhs4s = each(lambda at, rt: jnp.concatenate([at * m0, at * m1, rt * m0, rt * m1],
                                                axis=0).astype(BF16), ats, rts)
    rhs2s = each(lambda bt, kt: jnp.concatenate([bt, kt], axis=0).astype(BF16), bts, kts)
    gms = each(_dot_nt, lhs4s, rhs2s)
    pair = lambda gm, r0, c0: jnp.concatenate(
        [gm[r0:r0 + c, c0:c0 + c], gm[r0 + c:r0 + 2 * c, c0:c0 + c]], axis=1)
    lab_ws = [jnp.where(strict_w, pair(gm, 0, 0), 0.0) for gm in gms]
    lak_ws = [jnp.where(strict_w, pair(gm, 0, c), 0.0) for gm in gms]
    arb_ws = [jnp.where(incl_w, pair(gm, 2 * c, 0), 0.0) for gm in gms]
    ark_ws = [jnp.where(incl_w, pair(gm, 2 * c, c), 0.0) for gm in gms]

    t_ws = [eye_w + lab * lvl_ref[0] for lab in lab_ws]
    for lv in range(1, RWKV_LEVELS):
        lcs = [(lab * lvl_ref[lv]).astype(BF16) for lab in lab_ws]
        tbs = [t.astype(BF16) for t in t_ws]
        xws = each(_wide_dot, lcs, tbs)
        tx = each(lambda tb, xw: _wide_dot(tb, xw.astype(BF16)), tbs, xws)
        t_ws = each(lambda t, d: t + d, t_ws, tx)

    v01s = [jnp.concatenate([v * m0, v * m1], axis=0).astype(BF16) for v in vs]
    xvs = each(lambda lak, v01: _dot(lak.astype(BF16), v01), lak_ws, v01s)
    rhss = each(lambda at, xv: jnp.concatenate(
        [jnp.concatenate([at * m0, xv * m0], axis=1),
         jnp.concatenate([at * m1, xv * m1], axis=1)], axis=0).astype(BF16), ats, xvs)
    wus = each(lambda t, rhs: _dot(t.astype(BF16), rhs), t_ws, rhss)
    sps = each(lambda s, ref: s * jnp.exp(ref), s_prevs, refs)
    wss = each(lambda wu, rt, sp: _dot_nt(
        jnp.concatenate([wu[:, :LANE], rt], axis=0).astype(BF16), sp.astype(BF16)), wus, rts, sps)
    us = each(lambda ws, wu: ws[:c] + wu[:, LANE:], wss, wus)
    ys = each(lambda ws, arb, ark, u, v01: ws[c:] + _dot(
        jnp.concatenate([arb, ark], axis=1).astype(BF16),
        jnp.concatenate([(u * m0).astype(BF16), (u * m1).astype(BF16), v01], axis=0)),
        wss, arb_ws, ark_ws, us, v01s)
    upds = each(lambda u, v, rhs2: _dot_tn(jnp.concatenate([u, v], axis=0).astype(BF16), rhs2),
                us, vs, rhs2s)
    s_news = each(lambda sp, upd, e: (sp + jnp.where(bd, upd, 0.0)) * e[c - 1:c, :],
                  sps, upds, es)

    mus = _segsum64_stacked(ys, ones2)
    ycs = each(lambda y, mu: y - mu * (1.0 / HEAD), ys, mus)
    vars_ = _segsum64_stacked([yc * yc for yc in ycs], ones2)
    bsums = _segsum64_stacked(each(lambda r, k: r * k * rk, rs, ks), ones2)
    outs = each(lambda yc, var, bs, v, g: (yc * lax.rsqrt(var * (1.0 / HEAD) + GN_EPS) * gng + gnb
                                           + bs * v) * g, ycs, vars_, bsums, vs, gs)
    return outs, s_news


def _rwkv_kernel(r_ref, ld_ref, k_ref, v_ref, kk_ref, a_ref, g_ref, rk_ref, gng_ref, gnb_ref,
                 tri_ref, lvl_ref, ones2_ref, o_ref, s_ref, *, group):
    c = RWKV_CHUNK

    @pl.when(pl.program_id(1) == 0)
    def _():
        s_ref[...] = jnp.zeros_like(s_ref)

    row = lax.broadcasted_iota(jnp.int32, (c, 2 * c), 0)
    colw = lax.broadcasted_iota(jnp.int32, (c, 2 * c), 1) & (c - 1)
    strict_w = row > colw
    incl_w = row >= colw
    eye_w = (row == colw).astype(F32)
    r2 = lax.broadcasted_iota(jnp.int32, (LANE, LANE), 0)
    c2 = lax.broadcasted_iota(jnp.int32, (LANE, LANE), 1)
    bd = (r2 >= HEAD) == (c2 >= HEAD)
    lane = lax.broadcasted_iota(jnp.int32, (1, LANE), 1)
    m0 = (lane < HEAD).astype(F32)
    m1 = 1.0 - m0
    cst = (tri_ref[...], eye_w, lvl_ref, strict_w, incl_w, bd, ones2_ref[...], m0, m1,
           rk_ref[...], gng_ref[...], gnb_ref[...])
    in_refs = (r_ref, ld_ref, k_ref, v_ref, kk_ref, a_ref, g_ref)

    def body(i, carry):
        bs = [i * group + j for j in range(group)]
        outs, s_news = _rwkv_chunks([tuple(ref[b] for ref in in_refs) for b in bs],
                                    [s_ref[b] for b in bs], cst)
        for b, out, s_new in zip(bs, outs, s_news):
            s_ref[b] = s_new
            o_ref[b] = out.astype(o_ref.dtype)
        return carry

    lax.fori_loop(0, r_ref.shape[0] // group, body, 0)


def _rwkv(acts, rk, gng, gnb, consts, *, group):
    b, s, rd = acts[0].shape
    c = RWKV_CHUNK
    tri, lvl, ones2 = consts
    act = pl.BlockSpec((b, c, LANE), lambda p, t: (0, t, p))
    par = pl.BlockSpec((1, LANE), lambda p, t: (0, p))
    return pl.pallas_call(
        functools.partial(_rwkv_kernel, group=group),
        grid=(rd // LANE, s // c),
        in_specs=[act] * 7 + [par] * 3 + [_const_spec(tri.shape), _const_spec(lvl.shape),
                                          _const_spec(ones2.shape)],
        out_specs=act,
        out_shape=jax.ShapeDtypeStruct((b, s, rd), BF16),
        scratch_shapes=[pltpu.VMEM((b, LANE, LANE), F32)],
        compiler_params=pltpu.CompilerParams(
            dimension_semantics=("parallel", "arbitrary"), vmem_limit_bytes=VMEM_LIMIT_BYTES),
        name="rwkv",
    )(*acts, rk, gng, gnb, tri, lvl, ones2)


def _attn_kernel(q_ref, k_ref, v_ref, bias_ref, o_ref, s_scr, mx_scr, p_scr, acc_scr, m_scr,
                 alpha_scr, *, tq, tk, rb):
    nq = q_ref.shape[1] // tq
    ratio = tk // tq
    n_steps = sum(i // ratio + 1 for i in range(nq))
    lane = lax.broadcasted_iota(jnp.int32, (1, LANE), 1)
    heads = range(2)
    hs = lambda h: slice(h * LANE, (h + 1) * LANE)

    def advance(ij):
        i, j = ij
        last = j == i // ratio
        ni = jnp.where(last, jnp.minimum(i + 1, nq - 1), i)
        nj = jnp.where(last, jnp.where(i == nq - 1, j, 0), j + 1)
        return ni, nj

    def lane_block_max(sc):
        mx = sc[:, :LANE]
        for c0 in range(LANE, tk, LANE):
            mx = jnp.maximum(mx, sc[:, c0:c0 + LANE])
        return mx

    def scores(ij, slot, heads=heads):
        q0 = pl.multiple_of(ij[0] * tq, tq)
        k0 = pl.multiple_of(ij[1] * tk, tk)
        for h in heads:
            sc = _dot_nt(q_ref[0, pl.ds(q0, tq), hs(h)], k_ref[0, pl.ds(k0, tk), hs(h)])
            s_scr[slot, h] = sc
            mx_scr[slot, h] = lane_block_max(sc)

    def mask_diagonal(i, slot):
        bias = bias_ref[i % ratio]
        for h in heads:
            sc = s_scr[slot, h] + bias
            s_scr[slot, h] = sc
            mx_scr[slot, h] = lane_block_max(sc)

    def softmax(ij, slot, heads=heads):
        j = ij[1]
        cols = [pl.ds(c0, LANE) for c0 in range(0, tk, LANE)]
        for h in heads:
            for r0 in range(0, tq, rb):
                rows = pl.ds(r0, rb)
                mx = jnp.broadcast_to(
                    jnp.max(mx_scr[slot, h, rows, :], axis=1, keepdims=True), (rb, LANE))
                m_old = jnp.where(j == 0, -jnp.inf, m_scr[h, rows, :])
                m_new = jnp.maximum(m_old, mx)
                for cs in cols:
                    p_scr[h, rows, cs] = jnp.exp2(s_scr[slot, h, rows, cs] - m_new).astype(BF16)
                m_scr[h, rows, :] = m_new
                alpha_scr[h, rows, :] = jnp.exp2(m_old - m_new)

    def values(ij, heads=heads):
        k0 = pl.multiple_of(ij[1] * tk, tk)
        for h in heads:
            acc_scr[h] = (alpha_scr[h] * acc_scr[h]
                          + _dot(p_scr[h], v_ref[0, pl.ds(k0, tk), hs(h)]))

    def finalize(i):
        q0 = pl.multiple_of(i * tq, tq)
        a0, a1 = acc_scr[0], acc_scr[1]
        out = jnp.where(lane < HEAD, a0 / a0[:, HEAD:HEAD + 1], a1 / a1[:, 0:1])
        o_ref[0, pl.ds(q0, tq), :] = out.astype(o_ref.dtype)

    acc_scr[...] = jnp.zeros_like(acc_scr)
    m_scr[...] = jnp.full_like(m_scr, -jnp.inf)
    zero = jnp.int32(0)
    ij0 = (zero, zero)
    scores(ij0, 0)
    mask_diagonal(zero, 0)

    def step(carry, slot):
        ij_a, ij_b = carry
        for h in heads:
            scores(ij_b, 1 - slot, (h,))
            softmax(ij_a, slot, (h,))
            values(ij_a, (h,))

        @pl.when(ij_a[1] == ij_a[0] // ratio)
        def _():
            finalize(ij_a[0])

        @pl.when(ij_b[1] == ij_b[0] // ratio)
        def _():
            mask_diagonal(ij_b[0], 1 - slot)

        return ij_b, advance(ij_b)

    carry = (ij0, advance(ij0))
    first = n_steps % 2
    if first:
        carry = step(carry, 0)
    lax.fori_loop(0, n_steps // 2,
                  lambda _, c: step(step(c, first), 1 - first), carry)


def _attn(q, k, v, bias, *, tq, tk):
    b, s, hq = q.shape
    pairs = hq // (2 * LANE)
    blk = pl.BlockSpec((1, s, 2 * LANE), lambda i, p: (i, 0, p))
    return pl.pallas_call(
        functools.partial(_attn_kernel, tq=tq, tk=tk, rb=min(64, tq)),
        grid=(b, pairs),
        in_specs=[blk, blk, blk, _const_spec(bias.shape)],
        out_specs=pl.BlockSpec((1, s, LANE), lambda i, p: (i, 0, p)),
        out_shape=jax.ShapeDtypeStruct((b, s, pairs * LANE), BF16),
        scratch_shapes=[pltpu.VMEM((2, 2, tq, tk), F32), pltpu.VMEM((2, 2, tq, LANE), F32),
                        pltpu.VMEM((2, tq, tk), BF16),
                        pltpu.VMEM((2, tq, LANE), F32), pltpu.VMEM((2, tq, LANE), F32),
                        pltpu.VMEM((2, tq, LANE), F32)],
        compiler_params=pltpu.CompilerParams(
            dimension_semantics=("parallel", "parallel"), vmem_limit_bytes=VMEM_LIMIT_BYTES),
        name="attn",
    )(q, k, v, bias)


def _mix_ffn_ln_kernel(h_ref, yr_ref, om_ref, wgate_ref, wur_ref, wum_ref, wo_ref, g2_ref, b2_ref,
                       w1_ref, w3_ref, w2_ref, g3_ref, b3_ref, o_ref, *, alpha):
    h = h_ref[...]
    d = h.shape[1]
    gates = jax.nn.sigmoid(_dot(h.astype(BF16), wgate_ref[...]))
    y_r = _dot(yr_ref[...], wur_ref[...])
    y_m = _dot(om_ref[...], wum_ref[...])
    mixin = (gates[:, :d] * y_r + gates[:, d:] * y_m).astype(BF16)
    mix = _dot(mixin, wo_ref[...])
    h2 = _layer_norm(alpha * h + mix, g2_ref[...], b2_ref[...])
    o_ref[...] = _swiglu_ln(h2, w1_ref, w3_ref, w2_ref, g3_ref, b3_ref, alpha)


def _mix_ffn_ln(h2, yr, om, wts, *, alpha, tm):
    t, d = h2.shape
    tok = lambda n: pl.BlockSpec((tm, n), lambda i: (i, 0))
    return pl.pallas_call(
        functools.partial(_mix_ffn_ln_kernel, alpha=alpha),
        grid=(t // tm,),
        in_specs=[tok(d), tok(yr.shape[1]), tok(om.shape[1])] + [_const_spec(w.shape) for w in wts],
        out_specs=tok(d),
        out_shape=jax.ShapeDtypeStruct((t, d), F32),
        compiler_params=pltpu.CompilerParams(
            dimension_semantics=("parallel",), vmem_limit_bytes=VMEM_LIMIT_BYTES),
        name="mix_ffn_ln",
    )(h2, yr, om, *wts)


def _pad_cols(w, n):
    return jnp.pad(w, ((0, 0), (0, n - w.shape[1])))


def _rope_tables(s):
    inv_freq = ROPE_THETA ** (-jnp.arange(0, ROPE, 2, dtype=F32) / ROPE)
    ang = jnp.arange(s, dtype=F32)[:, None] * inv_freq[None, :]
    cos, sin = jnp.cos(ang), jnp.sin(ang)
    half = ROPE // 2
    z = lambda n: jnp.zeros((s, n), F32)
    cos_t = jnp.concatenate([jnp.ones((s, HEAD), F32), cos, cos, z(LANE - HEAD - ROPE)], axis=1)
    sin_a = jnp.concatenate([z(HEAD + half), sin, z(LANE - HEAD - ROPE)], axis=1)
    sin_b = jnp.concatenate([z(HEAD), -sin, z(LANE - HEAD - half)], axis=1)
    return cos_t, sin_a, sin_b


def _rwkv_consts():
    c = RWKV_CHUNK
    i = jnp.arange(c)[:, None]
    j = jnp.arange(c)[None, :]
    tri = (i >= j).astype(BF16)
    lv = []
    for lg in range(RWKV_LEVELS):
        m = ((i >> (lg + 1)) == (j >> (lg + 1))) & (((i >> lg) & 1) == 1) & (((j >> lg) & 1) == 0)
        lv.append(jnp.concatenate([m, m], axis=1).astype(F32))
    return tri, jnp.stack(lv)


def _ones2():
    i = jnp.arange(2 * LANE)[:, None] % LANE
    j = jnp.arange(LANE)[None, :]
    return ((i // HEAD) == (j // HEAD)).astype(BF16)


def _attn_bias(tq, tk):
    j = jnp.arange(tk)[None, :] // CHUNK
    tiles = []
    for v in range(tk // tq):
        i = (v * tq + jnp.arange(tq)[:, None]) // CHUNK
        tiles.append(jnp.where(j <= i, 0.0, -jnp.inf).astype(F32))
    return jnp.stack(tiles)


def kernel(x, ffn1_w1, ffn1_w3, ffn1_w2, ln1_g, ln1_b, w_in, mu_shift, w0, w_decay_up, a0, w_iclr_up, w_gate_up, k_k, k_a, r_k, gn_g, gn_b, q_norm_g, w_q_up, kv_norm_g, w_kv_up, w_up_rwkv, w_up_mla, w_o, ln2_g, ln2_b, ffn2_w1, ffn2_w3, ffn2_w2, ln3_g, ln3_b):
    b, s, d = x.shape
    depth = ffn1_w1.shape[0]
    rd = w0.shape[1]
    nh = d // LANE
    alpha = float((2.0 * depth) ** 0.25)
    f = ffn1_w1.shape[2]
    fp = -(-f // LANE) * LANE
    tm_ffn = min(256, b * s)
    tm_proj = min(512, s)
    tq = min(512, s)
    tk = min(2 * tq, s)
    row = lambda p: p.reshape(1, -1).astype(F32)

    tabs = _rope_tables(s)
    tri, lvl = _rwkv_consts()
    ones2 = _ones2()
    bias = _attn_bias(tq, tk)
    lane = jnp.arange(nh * LANE) % LANE
    head = jnp.arange(nh * LANE) // LANE
    vone = jnp.where(head % 2 == 0, lane == HEAD, lane == 0).astype(F32)[None, :]

    n_rkv = 3 * rd
    n_lora = n_rkv + DECAY_LORA + ICLR_LORA
    n_shift = n_lora + GATE_LORA
    n_shift_p = n_lora + 2 * LANE
    n_mla = n_shift + Q_LORA + KV_LORA

    h = x.reshape(b * s, d)
    for l in range(depth):
        ffn_wts = lambda w1, w3, w2, g, bb: (
            _pad_cols(w1[l], fp).astype(BF16), _pad_cols(w3[l], fp).astype(BF16),
            jnp.pad(w2[l], ((0, fp - f), (0, 0))).astype(BF16), row(g[l]), row(bb[l]))
        h = _ffn_ln(h, *ffn_wts(ffn1_w1, ffn1_w3, ffn1_w2, ln1_g, ln1_b), alpha=alpha,
                    tm=min(512, b * s))

        w = w_in[l]
        z = lambda n: jnp.zeros((d, n), F32)
        kpe_blk = jnp.concatenate([z(HEAD), w[:, n_mla:n_mla + ROPE], z(LANE - HEAD - ROPE)], axis=1)
        win = jnp.concatenate([w[:, :n_shift], z(n_shift_p - n_shift), w[:, n_shift:n_mla], kpe_blk],
                              axis=1).astype(BF16)
        mu = jnp.pad(mu_shift[l], (0, n_shift_p - n_shift))[None, :]
        zl = jnp.zeros((DECAY_LORA, rd), F32)
        wda = jnp.concatenate([jnp.concatenate([w_decay_up[l], zl], axis=1),
                               jnp.concatenate([zl, w_iclr_up[l]], axis=1)], axis=0).astype(BF16)
        wg = jnp.pad(w_gate_up[l], ((0, 2 * LANE - GATE_LORA), (0, 0))).astype(BF16)
        wq = jnp.pad(w_q_up[l].reshape(Q_LORA, nh, QK_HEAD),
                     ((0, 0), (0, 0), (0, LANE - QK_HEAD))).reshape(Q_LORA, nh * LANE).astype(BF16)
        wkv3 = w_kv_up[l].reshape(KV_LORA, nh, 2 * HEAD)
        zk = jnp.zeros((KV_LORA, nh, HEAD), F32)
        wk = jnp.concatenate([wkv3[:, :, :HEAD], zk], axis=2).reshape(KV_LORA, nh * LANE)
        wv_even = jnp.concatenate([wkv3[:, :, HEAD:], zk], axis=2)
        wv_odd = jnp.concatenate([zk, wkv3[:, :, HEAD:]], axis=2)
        odd = (jnp.arange(nh) % 2 == 1)[None, :, None]
        wv = jnp.where(odd, wv_odd, wv_even).reshape(KV_LORA, nh * LANE)
        wkv = jnp.concatenate([wk, wv], axis=1).astype(BF16)
        wts = (win, mu, wda, row(w0[l]), row(a0[l]), wg, row(k_k[l]), row(k_a[l]), ones2,
               row(q_norm_g[l]), wq, row(kv_norm_g[l]), wkv, vone)
        r, ld, km, v, kk, a, g, q, kx, vx = _proj(h.reshape(b, s, d), tabs, wts, tm=tm_proj)

        yr = _rwkv((r, ld, km, v, kk, a, g), row(r_k[l]), row(gn_g[l]), row(gn_b[l]),
                   (tri, lvl, ones2), group=min(8, b))
        om = _attn(q, kx, vx, bias, tq=tq, tk=tk)

        mix_wts = (w[:, n_mla + ROPE:].astype(BF16), w_up_rwkv[l].astype(BF16),
                   w_up_mla[l].astype(BF16), w_o[l].astype(BF16), row(ln2_g[l]), row(ln2_b[l]))
        h = _mix_ffn_ln(h, yr.reshape(b * s, rd), om.reshape(b * s, -1),
                        mix_wts + ffn_wts(ffn2_w1, ffn2_w3, ffn2_w2, ln3_g, ln3_b),
                        alpha=alpha, tm=tm_ffn)
    return h.reshape(b, s, d)
```

```python
import functools

import jax
import jax.numpy as jnp
from jax import lax
from jax.experimental import pallas as pl
from jax.experimental.pallas import tpu as pltpu

F32 = jnp.float32
BF16 = jnp.bfloat16

LANE = 128
SUBLANE = 8
VMEM_LIMIT_BYTES = 56 * 1024 * 1024

HEAD = 64
ROPE = 32
QK_HEAD = HEAD + ROPE
CHUNK = 64
DECAY_LORA = 64
ICLR_LORA = 64
GATE_LORA = 160
Q_LORA = 384
KV_LORA = 256
GN_EPS = 64e-5
LN_EPS = 1e-5
RMS_EPS = 1e-6
ROPE_THETA = 10000.0
LOG2E = 1.4426950408889634

MASK_LANE0 = HEAD + ROPE
MASK_NEG = -1e30

RWKV_CHUNK = 128
RWKV_LEVELS = 7


def _dot(a, b):
    return jnp.dot(a, b, preferred_element_type=F32)


def _dot_nt(a, b):
    return lax.dot_general(a, b, (((1,), (1,)), ((), ())), preferred_element_type=F32)


def _dot_tn(a, b):
    return lax.dot_general(a, b, (((0,), (0,)), ((), ())), preferred_element_type=F32)


def _split2(x):
    hi = x.astype(BF16)
    lo = (x - hi.astype(F32)).astype(BF16)
    return hi, lo


def _split3(x):
    hi = x.astype(BF16)
    r1 = x - hi.astype(F32)
    mid = r1.astype(BF16)
    lo = (r1 - mid.astype(F32)).astype(BF16)
    return hi, mid, lo


def _layer_norm(z, g, b):
    mu = jnp.mean(z, axis=-1, keepdims=True)
    zc = z - mu
    var = jnp.mean(zc * zc, axis=-1, keepdims=True)
    return zc * lax.rsqrt(var + LN_EPS) * g + b


def _rms_norm(z, g):
    return z * lax.rsqrt(jnp.mean(z * z, axis=-1, keepdims=True) + RMS_EPS) * g


def _softplus(u):
    return jnp.maximum(u, 0.0) + jnp.log(1.0 + jnp.exp(-jnp.abs(u)))


def _const_spec(shape):
    nd = len(shape)
    return pl.BlockSpec(shape, lambda *_: (0,) * nd, pipeline_mode=pl.Buffered(1))


def _swiglu_ln(x, w1_ref, w3_ref, w2_ref, g_ref, b_ref, alpha):
    xb = x.astype(BF16)
    h1 = _dot(xb, w1_ref[...])
    h3 = _dot(xb, w3_ref[...])
    act = (h1 * jax.nn.sigmoid(h1) * h3).astype(BF16)
    y = _dot(act, w2_ref[...])
    return _layer_norm(alpha * x + 0.5 * y, g_ref[...], b_ref[...])


def _ffn_ln_kernel(x_ref, w1_ref, w3_ref, w2_ref, g_ref, b_ref, o_ref, *, alpha):
    o_ref[...] = _swiglu_ln(x_ref[...], w1_ref, w3_ref, w2_ref, g_ref, b_ref, alpha)


def _ffn_ln(x2, w1, w3, w2, g, b, *, alpha, tm):
    t, d = x2.shape
    f = w1.shape[1]
    return pl.pallas_call(
        functools.partial(_ffn_ln_kernel, alpha=alpha),
        grid=(t // tm,),
        in_specs=[
            pl.BlockSpec((tm, d), lambda i: (i, 0)),
            _const_spec((d, f)), _const_spec((d, f)), _const_spec((f, d)),
            _const_spec((1, d)), _const_spec((1, d)),
        ],
        out_specs=pl.BlockSpec((tm, d), lambda i: (i, 0)),
        out_shape=jax.ShapeDtypeStruct((t, d), F32),
        compiler_params=pltpu.CompilerParams(
            dimension_semantics=("parallel",), vmem_limit_bytes=VMEM_LIMIT_BYTES),
        name="ffn_ln",
    )(x2, w1, w3, w2, g, b)


def _segsum64(x, ones2):
    outs = []
    for c in range(x.shape[1] // LANE):
        hi, lo = _split2(x[:, c * LANE:(c + 1) * LANE])
        outs.append(_dot(jnp.concatenate([hi, lo], axis=1), ones2))
    return outs[0] if len(outs) == 1 else jnp.concatenate(outs, axis=1)


def _rope(x, cos_t, sin_a, sin_b):
    return (x * cos_t + pltpu.roll(x, 16, axis=1) * sin_a
            + pltpu.roll(x, LANE - 16, axis=1) * sin_b)


def _proj_kernel(h_ref, cos_ref, sa_ref, sb_ref, km_ref, win_ref, mu_ref, wda_ref, w0_ref, a0_ref,
                 wg_ref, kk_ref, ka_ref, ones2_ref, qg_ref, wq_ref, kvg_ref, wkv_ref, vone_ref,
                 r_o, ld_o, k_o, v_o, kk_o, a_o, g_o, q_o, kx_o, vx_o, carry_ref,
                 *, rd, n_shift, q_scale):
    s = pl.program_id(1)
    tm = h_ref.shape[1]
    hb = h_ref[0].astype(BF16)
    proj = _dot(hb, win_ref[...])

    ps = proj[:, :n_shift]
    prev = pltpu.roll(ps, 1, axis=0)
    first = jnp.where(s == 0, 0.0, carry_ref[SUBLANE - 1:SUBLANE, :])
    rowid = lax.broadcasted_iota(jnp.int32, (tm, 1), 0)
    prev = jnp.where(rowid == 0, first, prev)
    carry_ref[...] = ps[tm - SUBLANE:, :]
    ps = ps + mu_ref[...] * (prev - ps)

    r = ps[:, 0:rd]
    k = ps[:, rd:2 * rd]
    v = ps[:, 2 * rd:3 * rd]
    wa = ps[:, 3 * rd:3 * rd + LANE]
    gd = ps[:, 3 * rd + LANE:n_shift]

    lane = lax.broadcasted_iota(jnp.int32, (1, LANE), 1)
    t_in = jnp.where(lane < DECAY_LORA, jnp.tanh(wa), wa).astype(BF16)
    za = _dot(t_in, wda_ref[...])
    z = w0_ref[...] + za[:, :rd]
    w = -_softplus(-z) - 0.5
    ld_o[0] = -jnp.exp(w)
    a = jax.nn.sigmoid(a0_ref[...] + za[:, rd:])
    g_o[0] = _dot(jax.nn.sigmoid(gd).astype(BF16), wg_ref[...])

    kkr = k * kk_ref[...]
    ss = _segsum64(kkr * kkr, ones2_ref[...])
    kk_o[0] = kkr * lax.rsqrt(jnp.maximum(ss, 1e-24))
    k_o[0] = k * (1.0 + (a - 1.0) * ka_ref[...])
    r_o[0] = r
    v_o[0] = v
    a_o[0] = a

    pm = proj[:, n_shift:]
    q_lat = pm[:, :Q_LORA]
    kv_lat = pm[:, Q_LORA:Q_LORA + KV_LORA]
    kp = pm[:, Q_LORA + KV_LORA:]
    cos_t, sin_a, sin_b = cos_ref[...], sa_ref[...], sb_ref[...]
    q = _dot(_rms_norm(q_lat, qg_ref[...]).astype(BF16), wq_ref[...])
    kv = _dot(_rms_norm(kv_lat, kvg_ref[...]).astype(BF16), wkv_ref[...])
    kpr = _rope(kp, cos_t, sin_a, sin_b) + km_ref[...]
    nh = q.shape[1] // LANE
    for h in range(nh):
        sl = slice(h * LANE, (h + 1) * LANE)
        q_o[0, :, sl] = (_rope(q[:, sl], cos_t, sin_a, sin_b) * q_scale).astype(BF16)
        kx_o[0, :, sl] = (kv[:, sl] + kpr).astype(BF16)
    vx_o[0] = (kv[:, nh * LANE:] + vone_ref[...]).astype(BF16)


def _proj(h3, tabs, wts, *, tm):
    b, s, d = h3.shape
    (win, mu, wda, w0, a0, wg, kkw, kaw, ones2, qg, wq, kvg, wkv, vone) = wts
    rd = w0.shape[1]
    n_shift = mu.shape[1]
    hq = wq.shape[1]
    tok = lambda n: pl.BlockSpec((1, tm, n), lambda i, j: (i, j, 0))
    tab = pl.BlockSpec((tm, LANE), lambda i, j: (j, 0))
    f32o = jax.ShapeDtypeStruct((b, s, rd), F32)
    bfo = jax.ShapeDtypeStruct((b, s, hq), BF16)
    return pl.pallas_call(
        functools.partial(_proj_kernel, rd=rd, n_shift=n_shift,
                          q_scale=float(QK_HEAD ** -0.5 * LOG2E)),
        grid=(b, s // tm),
        in_specs=[tok(d), tab, tab, tab, tab] + [_const_spec(w.shape) for w in wts],
        out_specs=[tok(rd)] * 7 + [tok(hq)] * 3,
        out_shape=[f32o] * 7 + [bfo] * 3,
        scratch_shapes=[pltpu.VMEM((SUBLANE, n_shift), F32)],
        compiler_params=pltpu.CompilerParams(
            dimension_semantics=("parallel", "arbitrary"), vmem_limit_bytes=VMEM_LIMIT_BYTES),
        name="proj",
    )(h3, *tabs, *wts)


def _wide_dot(a_w, b_w):
    c = a_w.shape[0]
    z = jnp.zeros((c, c), b_w.dtype)
    rhs = jnp.concatenate([jnp.concatenate([b_w[:, :c], z], axis=1),
                           jnp.concatenate([z, b_w[:, c:]], axis=1)], axis=0)
    return _dot(a_w, rhs)


def _rows(x, n):
    m = x.shape[0] // n
    return [x[i * m:(i + 1) * m] for i in range(n)]


def _segsum64_stacked(xs, ones2):
    parts = [jnp.concatenate(_split2(x), axis=1) for x in xs]
    return _rows(_dot(jnp.concatenate(parts, axis=0), ones2), len(xs))


def _rwkv_chunks(xs, s_prevs, cst):
    (tri, eye_w, lvl_ref, strict_w, incl_w, bd, ones2, m0, m1, rk, gng, gnb) = cst
    c = RWKV_CHUNK
    half = c // 2
    n = len(xs)
    each = lambda f, *ls: [f(*a) for a in zip(*ls)]
    rs, lds, ks, vs, kks, as_, gs = (list(t) for t in zip(*xs))

    ld3 = jnp.concatenate([jnp.concatenate(_split3(ld), axis=1) for ld in lds], axis=1)
    cl3 = _dot(tri, ld3)
    cls = [cl3[:, (3 * i) * LANE:(3 * i + 1) * LANE] + cl3[:, (3 * i + 1) * LANE:(3 * i + 2) * LANE]
           + cl3[:, (3 * i + 2) * LANE:(3 * i + 3) * LANE] for i in range(n)]
    refs = [cl[half - 1:half, :] for cl in cls]
    es = each(lambda cl, ref: jnp.exp(cl - ref), cls, refs)
    einvs = each(lambda cl, ref: jnp.exp(ref - cl), cls, refs)
    rts = each(lambda r, e: r * e, rs, es)
    ats = each(lambda kk, e, ld: -(kk * e) * jnp.exp(-ld), kks, es, lds)
    bts = each(lambda kk, a, ei: kk * a * ei, kks, as_, einvs)
    kts = each(lambda k, ei: k * ei, ks, einvs)

    lhs4s = each(lambda at, rt: jnp.concatenate([at * m0, at * m1, rt * m0, rt * m1],
                                                axis=0).astype(BF16), ats, rts)
    rhs2s = each(lambda bt, kt: jnp.concatenate([bt, kt], axis=0).astype(BF16), bts, kts)
    gms = each(_dot_nt, lhs4s, rhs2s)
    pair = lambda gm, r0, c0: jnp.concatenate(
        [gm[r0:r0 + c, c0:c0 + c], gm[r0 + c:r0 + 2 * c, c0:c0 + c]], axis=1)
    lab_ws = [jnp.where(strict_w, pair(gm, 0, 0), 0.0) for gm in gms]
    lak_ws = [jnp.where(strict_w, pair(gm, 0, c), 0.0) for gm in gms]
    arb_ws = [jnp.where(incl_w, pair(gm, 2 * c, 0), 0.0) for gm in gms]
    ark_ws = [jnp.where(incl_w, pair(gm, 2 * c, c), 0.0) for gm in gms]

    t_ws = [eye_w + lab * lvl_ref[0] for lab in lab_ws]
    for lv in range(1, RWKV_LEVELS):
        lcs = [(lab * lvl_ref[lv]).astype(BF16) for lab in lab_ws]
        tbs = [t.astype(BF16) for t in t_ws]
        xws = each(_wide_dot, lcs, tbs)
        tx = each(lambda tb, xw: _wide_dot(tb, xw.astype(BF16)), tbs, xws)
        t_ws = each(lambda t, d: t + d, t_ws, tx)

    v01s = [jnp.concatenate([v * m0, v * m1], axis=0).astype(BF16) for v in vs]
    xvs = each(lambda lak, v01: _dot(lak.astype(BF16), v01), lak_ws, v01s)
    rhss = each(lambda at, xv: jnp.concatenate(
        [jnp.concatenate([at * m0, xv * m0], axis=1),
         jnp.concatenate([at * m1, xv * m1], axis=1)], axis=0).astype(BF16), ats, xvs)
    wus = each(lambda t, rhs: _dot(t.astype(BF16), rhs), t_ws, rhss)
    sps = each(lambda s, ref: s * jnp.exp(ref), s_prevs, refs)
    wss = each(lambda wu, rt, sp: _dot_nt(
        jnp.concatenate([wu[:, :LANE], rt], axis=0).astype(BF16), sp.astype(BF16)), wus, rts, sps)
    us = each(lambda ws, wu: ws[:c] + wu[:, LANE:], wss, wus)
    ys = each(lambda ws, arb, ark, u, v01: ws[c:] + _dot(
        jnp.concatenate([arb, ark], axis=1).astype(BF16),
        jnp.concatenate([(u * m0).astype(BF16), (u * m1).astype(BF16), v01], axis=0)),
        wss, arb_ws, ark_ws, us, v01s)
    upds = each(lambda u, v, rhs2: _dot_tn(jnp.concatenate([u, v], axis=0).astype(BF16), rhs2),
                us, vs, rhs2s)
    s_news = each(lambda sp, upd, e: (sp + jnp.where(bd, upd, 0.0)) * e[c - 1:c, :],
                  sps, upds, es)

    mus = _segsum64_stacked(ys, ones2)
    ycs = each(lambda y, mu: y - mu * (1.0 / HEAD), ys, mus)
    vars_ = _segsum64_stacked([yc * yc for yc in ycs], ones2)
    bsums = _segsum64_stacked(each(lambda r, k: r * k * rk, rs, ks), ones2)
    outs = each(lambda yc, var, bs, v, g: (yc * lax.rsqrt(var * (1.0 / HEAD) + GN_EPS) * gng + gnb
                                           + bs * v) * g, ycs, vars_, bsums, vs, gs)
    return outs, s_news


def _rwkv_kernel(r_ref, ld_ref, k_ref, v_ref, kk_ref, a_ref, g_ref, rk_ref, gng_ref, gnb_ref,
                 tri_ref, lvl_ref, ones2_ref, o_ref, s_ref, *, group):
    c = RWKV_CHUNK

    @pl.when(pl.program_id(1) == 0)
    def _():
        s_ref[...] = jnp.zeros_like(s_ref)

    row = lax.broadcasted_iota(jnp.int32, (c, 2 * c), 0)
    colw = lax.broadcasted_iota(jnp.int32, (c, 2 * c), 1) & (c - 1)
    strict_w = row > colw
    incl_w = row >= colw
    eye_w = (row == colw).astype(F32)
    r2 = lax.broadcasted_iota(jnp.int32, (LANE, LANE), 0)
    c2 = lax.broadcasted_iota(jnp.int32, (LANE, LANE), 1)
    bd = (r2 >= HEAD) == (c2 >= HEAD)
    lane = lax.broadcasted_iota(jnp.int32, (1, LANE), 1)
    m0 = (lane < HEAD).astype(F32)
    m1 = 1.0 - m0
    cst = (tri_ref[...], eye_w, lvl_ref, strict_w, incl_w, bd, ones2_ref[...], m0, m1,
           rk_ref[...], gng_ref[...], gnb_ref[...])
    in_refs = (r_ref, ld_ref, k_ref, v_ref, kk_ref, a_ref, g_ref)

    def body(i, carry):
        bs = [i * group + j for j in range(group)]
        outs, s_news = _rwkv_chunks([tuple(ref[b] for ref in in_refs) for b in bs],
                                    [s_ref[b] for b in bs], cst)
        for b, out, s_new in zip(bs, outs, s_news):
            s_ref[b] = s_new
            o_ref[b] = out.astype(o_ref.dtype)
        return carry

    lax.fori_loop(0, r_ref.shape[0] // group, body, 0)


def _rwkv(acts, rk, gng, gnb, consts, *, group):
    b, s, rd = acts[0].shape
    c = RWKV_CHUNK
    tri, lvl, ones2 = consts
    act = pl.BlockSpec((b, c, LANE), lambda p, t: (0, t, p))
    par = pl.BlockSpec((1, LANE), lambda p, t: (0, p))
    return pl.pallas_call(
        functools.partial(_rwkv_kernel, group=group),
        grid=(rd // LANE, s // c),
        in_specs=[act] * 7 + [par] * 3 + [_const_spec(tri.shape), _const_spec(lvl.shape),
                                          _const_spec(ones2.shape)],
        out_specs=act,
        out_shape=jax.ShapeDtypeStruct((b, s, rd), BF16),
        scratch_shapes=[pltpu.VMEM((b, LANE, LANE), F32)],
        compiler_params=pltpu.CompilerParams(
            dimension_semantics=("parallel", "arbitrary"), vmem_limit_bytes=VMEM_LIMIT_BYTES),
        name="rwkv",
    )(*acts, rk, gng, gnb, tri, lvl, ones2)


def _attn_kernel(q_ref, k_ref, v_ref, qm_ref, o_ref, s_scr, mx_scr, acc_scr, m_scr,
                 alpha_scr, *, tq, tk, rb):
    nq = q_ref.shape[1] // tq
    ratio = tk // tq
    n_steps = sum(i // ratio + 1 for i in range(nq))
    lane = lax.broadcasted_iota(jnp.int32, (1, LANE), 1)
    heads = range(2)
    hs = lambda h: slice(h * LANE, (h + 1) * LANE)

    def advance(ij):
        i, j = ij
        last = j == i // ratio
        ni = jnp.where(last, jnp.minimum(i + 1, nq - 1), i)
        nj = jnp.where(last, jnp.where(i == nq - 1, j, 0), j + 1)
        return ni, nj

    def lane_block_max(sc):
        mx = sc[:, :LANE]
        for c0 in range(LANE, tk, LANE):
            mx = jnp.maximum(mx, sc[:, c0:c0 + LANE])
        return mx

    def scores(ij, slot, heads=heads):
        i, j = ij
        q0 = pl.multiple_of(i * tq, tq)
        k0 = pl.multiple_of(j * tk, tk)
        qm = qm_ref[jnp.where(j == i // ratio, 1 + i % ratio, 0)]
        for h in heads:
            sc = _dot_nt(q_ref[0, pl.ds(q0, tq), hs(h)] + qm, k_ref[0, pl.ds(k0, tk), hs(h)])
            s_scr[slot, h] = sc
            mx_scr[slot, h] = lane_block_max(sc)

    def softmax(ij, slot, heads=heads):
        j = ij[1]
        cols = [pl.ds(c0, LANE) for c0 in range(0, tk, LANE)]
        ps = []
        for h in heads:
            blocks = []
            for r0 in range(0, tq, rb):
                rows = pl.ds(r0, rb)
                mx = jnp.broadcast_to(
                    jnp.max(mx_scr[slot, h, rows, :], axis=1, keepdims=True), (rb, LANE))
                m_old = jnp.where(j == 0, -jnp.inf, m_scr[h, rows, :])
                m_new = jnp.maximum(m_old, mx)
                blocks.append(jnp.concatenate(
                    [jnp.exp2(s_scr[slot, h, rows, cs] - m_new).astype(BF16) for cs in cols], axis=1))
                m_scr[h, rows, :] = m_new
                alpha_scr[h, rows, :] = jnp.exp2(m_old - m_new)
            ps.append(jnp.concatenate(blocks, axis=0))
        return ps

    def values(ij, ps, heads=heads):
        k0 = pl.multiple_of(ij[1] * tk, tk)
        for h, p in zip(heads, ps):
            acc_scr[h] = (alpha_scr[h] * acc_scr[h]
                          + _dot(p, v_ref[0, pl.ds(k0, tk), hs(h)]))

    def finalize(i):
        q0 = pl.multiple_of(i * tq, tq)
        a0, a1 = acc_scr[0], acc_scr[1]
        out = jnp.where(lane < HEAD, a0 / a0[:, HEAD:HEAD + 1], a1 / a1[:, 0:1])
        o_ref[0, pl.ds(q0, tq), :] = out.astype(o_ref.dtype)

    acc_scr[...] = jnp.zeros_like(acc_scr)
    m_scr[...] = jnp.full_like(m_scr, -jnp.inf)
    zero = jnp.int32(0)
    ij0 = (zero, zero)
    scores(ij0, 0)

    def step(carry, slot):
        ij_a, ij_b = carry
        for h in heads:
            scores(ij_b, 1 - slot, (h,))
            values(ij_a, softmax(ij_a, slot, (h,)), (h,))

        @pl.when(ij_a[1] == ij_a[0] // ratio)
        def _():
            finalize(ij_a[0])

        return ij_b, advance(ij_b)

    carry = (ij0, advance(ij0))
    first = n_steps % 2
    if first:
        carry = step(carry, 0)
    lax.fori_loop(0, n_steps // 2,
                  lambda _, c: step(step(c, first), 1 - first), carry)


def _attn(q, k, v, qm, *, tq, tk):
    b, s, hq = q.shape
    pairs = hq // (2 * LANE)
    blk = pl.BlockSpec((1, s, 2 * LANE), lambda i, p: (i, 0, p))
    return pl.pallas_call(
        functools.partial(_attn_kernel, tq=tq, tk=tk, rb=min(64, tq)),
        grid=(b, pairs),
        in_specs=[blk, blk, blk, _const_spec(qm.shape)],
        out_specs=pl.BlockSpec((1, s, LANE), lambda i, p: (i, 0, p)),
        out_shape=jax.ShapeDtypeStruct((b, s, pairs * LANE), BF16),
        scratch_shapes=[pltpu.VMEM((2, 2, tq, tk), F32), pltpu.VMEM((2, 2, tq, LANE), F32),
                        pltpu.VMEM((2, tq, LANE), F32), pltpu.VMEM((2, tq, LANE), F32),
                        pltpu.VMEM((2, tq, LANE), F32)],
        compiler_params=pltpu.CompilerParams(
            dimension_semantics=("parallel", "parallel"), vmem_limit_bytes=VMEM_LIMIT_BYTES),
        name="attn",
    )(q, k, v, qm)


def _mix_ffn_ln_kernel(h_ref, yr_ref, om_ref, wgate_ref, wur_ref, wum_ref, wo_ref, g2_ref, b2_ref,
                       w1_ref, w3_ref, w2_ref, g3_ref, b3_ref, o_ref, *, alpha):
    h = h_ref[...]
    d = h.shape[1]
    gates = jax.nn.sigmoid(_dot(h.astype(BF16), wgate_ref[...]))
    y_r = _dot(yr_ref[...], wur_ref[...])
    y_m = _dot(om_ref[...], wum_ref[...])
    mixin = (gates[:, :d] * y_r + gates[:, d:] * y_m).astype(BF16)
    mix = _dot(mixin, wo_ref[...])
    h2 = _layer_norm(alpha * h + mix, g2_ref[...], b2_ref[...])
    o_ref[...] = _swiglu_ln(h2, w1_ref, w3_ref, w2_ref, g3_ref, b3_ref, alpha)


def _mix_ffn_ln(h2, yr, om, wts, *, alpha, tm):
    t, d = h2.shape
    tok = lambda n: pl.BlockSpec((tm, n), lambda i: (i, 0))
    return pl.pallas_call(
        functools.partial(_mix_ffn_ln_kernel, alpha=alpha),
        grid=(t // tm,),
        in_specs=[tok(d), tok(yr.shape[1]), tok(om.shape[1])] + [_const_spec(w.shape) for w in wts],
        out_specs=tok(d),
        out_shape=jax.ShapeDtypeStruct((t, d), F32),
        compiler_params=pltpu.CompilerParams(
            dimension_semantics=("parallel",), vmem_limit_bytes=VMEM_LIMIT_BYTES),
        name="mix_ffn_ln",
    )(h2, yr, om, *wts)


def _pad_cols(w, n):
    return jnp.pad(w, ((0, 0), (0, n - w.shape[1])))


def _rope_tables(s):
    inv_freq = ROPE_THETA ** (-jnp.arange(0, ROPE, 2, dtype=F32) / ROPE)
    ang = jnp.arange(s, dtype=F32)[:, None] * inv_freq[None, :]
    cos, sin = jnp.cos(ang), jnp.sin(ang)
    half = ROPE // 2
    z = lambda n: jnp.zeros((s, n), F32)
    cos_t = jnp.concatenate([jnp.ones((s, HEAD), F32), cos, cos, z(LANE - HEAD - ROPE)], axis=1)
    sin_a = jnp.concatenate([z(HEAD + half), sin, z(LANE - HEAD - ROPE)], axis=1)
    sin_b = jnp.concatenate([z(HEAD), -sin, z(LANE - HEAD - half)], axis=1)
    return cos_t, sin_a, sin_b


def _rwkv_consts():
    c = RWKV_CHUNK
    i = jnp.arange(c)[:, None]
    j = jnp.arange(c)[None, :]
    tri = (i >= j).astype(BF16)
    lv = []
    for lg in range(RWKV_LEVELS):
        m = ((i >> (lg + 1)) == (j >> (lg + 1))) & (((i >> lg) & 1) == 1) & (((j >> lg) & 1) == 0)
        lv.append(jnp.concatenate([m, m], axis=1).astype(F32))
    return tri, jnp.stack(lv)


def _ones2():
    i = jnp.arange(2 * LANE)[:, None] % LANE
    j = jnp.arange(LANE)[None, :]
    return ((i // HEAD) == (j // HEAD)).astype(BF16)


def _mask_tables(s, tq, tk):
    nc = tk // CHUNK
    assert MASK_LANE0 + nc <= LANE
    lane = jnp.arange(LANE)[None, :] - MASK_LANE0
    kc = (jnp.arange(s)[:, None] % tk) // CHUNK
    kmask = jnp.where((lane >= 0) & (lane < nc) & (kc > lane), MASK_NEG, 0.0).astype(F32)
    qm = [jnp.zeros((tq, LANE), F32)]
    for v in range(tk // tq):
        qc = (v * tq + jnp.arange(tq)[:, None]) // CHUNK
        qm.append((lane == qc).astype(F32))
    return kmask, jnp.stack(qm).astype(BF16)


def kernel(x, ffn1_w1, ffn1_w3, ffn1_w2, ln1_g, ln1_b, w_in, mu_shift, w0, w_decay_up, a0, w_iclr_up, w_gate_up, k_k, k_a, r_k, gn_g, gn_b, q_norm_g, w_q_up, kv_norm_g, w_kv_up, w_up_rwkv, w_up_mla, w_o, ln2_g, ln2_b, ffn2_w1, ffn2_w3, ffn2_w2, ln3_g, ln3_b):
    b, s, d = x.shape
    depth = ffn1_w1.shape[0]
    rd = w0.shape[1]
    nh = d // LANE
    alpha = float((2.0 * depth) ** 0.25)
    f = ffn1_w1.shape[2]
    fp = -(-f // LANE) * LANE
    tm_ffn = min(256, b * s)
    tm_proj = min(512, s)
    tq = min(512, s)
    tk = min(2 * tq, s)
    row = lambda p: p.reshape(1, -1).astype(F32)

    kmask, qm = _mask_tables(s, tq, tk)
    tabs = _rope_tables(s) + (kmask,)
    tri, lvl = _rwkv_consts()
    ones2 = _ones2()
    lane = jnp.arange(nh * LANE) % LANE
    head = jnp.arange(nh * LANE) // LANE
    vone = jnp.where(head % 2 == 0, lane == HEAD, lane == 0).astype(F32)[None, :]

    n_rkv = 3 * rd
    n_lora = n_rkv + DECAY_LORA + ICLR_LORA
    n_shift = n_lora + GATE_LORA
    n_shift_p = n_lora + 2 * LANE
    n_mla = n_shift + Q_LORA + KV_LORA

    h = x.reshape(b * s, d)
    for l in range(depth):
        ffn_wts = lambda w1, w3, w2, g, bb: (
            _pad_cols(w1[l], fp).astype(BF16), _pad_cols(w3[l], fp).astype(BF16),
            jnp.pad(w2[l], ((0, fp - f), (0, 0))).astype(BF16), row(g[l]), row(bb[l]))
        h = _ffn_ln(h, *ffn_wts(ffn1_w1, ffn1_w3, ffn1_w2, ln1_g, ln1_b), alpha=alpha,
                    tm=min(512, b * s))

        w = w_in[l]
        z = lambda n: jnp.zeros((d, n), F32)
        kpe_blk = jnp.concatenate([z(HEAD), w[:, n_mla:n_mla + ROPE], z(LANE - HEAD - ROPE)], axis=1)
        win = jnp.concatenate([w[:, :n_shift], z(n_shift_p - n_shift), w[:, n_shift:n_mla], kpe_blk],
                              axis=1).astype(BF16)
        mu = jnp.pad(mu_shift[l], (0, n_shift_p - n_shift))[None, :]
        zl = jnp.zeros((DECAY_LORA, rd), F32)
        wda = jnp.concatenate([jnp.concatenate([w_decay_up[l], zl], axis=1),
                               jnp.concatenate([zl, w_iclr_up[l]], axis=1)], axis=0).astype(BF16)
        wg = jnp.pad(w_gate_up[l], ((0, 2 * LANE - GATE_LORA), (0, 0))).astype(BF16)
        wq = jnp.pad(w_q_up[l].reshape(Q_LORA, nh, QK_HEAD),
                     ((0, 0), (0, 0), (0, LANE - QK_HEAD))).reshape(Q_LORA, nh * LANE).astype(BF16)
        wkv3 = w_kv_up[l].reshape(KV_LORA, nh, 2 * HEAD)
        zk = jnp.zeros((KV_LORA, nh, HEAD), F32)
        wk = jnp.concatenate([wkv3[:, :, :HEAD], zk], axis=2).reshape(KV_LORA, nh * LANE)
        wv_even = jnp.concatenate([wkv3[:, :, HEAD:], zk], axis=2)
        wv_odd = jnp.concatenate([zk, wkv3[:, :, HEAD:]], axis=2)
        odd = (jnp.arange(nh) % 2 == 1)[None, :, None]
        wv = jnp.where(odd, wv_odd, wv_even).reshape(KV_LORA, nh * LANE)
        wkv = jnp.concatenate([wk, wv], axis=1).astype(BF16)
        wts = (win, mu, wda, row(w0[l]), row(a0[l]), wg, row(k_k[l]), row(k_a[l]), ones2,
               row(q_norm_g[l]), wq, row(kv_norm_g[l]), wkv, vone)
        r, ld, km, v, kk, a, g, q, kx, vx = _proj(h.reshape(b, s, d), tabs, wts, tm=tm_proj)

        yr = _rwkv((r, ld, km, v, kk, a, g), row(r_k[l]), row(gn_g[l]), row(gn_b[l]),
                   (tri, lvl, ones2), group=min(8, b))
        om = _attn(q, kx, vx, qm, tq=tq, tk=tk)

        mix_wts = (w[:, n_mla + ROPE:].astype(BF16), w_up_rwkv[l].astype(BF16),
                   w_up_mla[l].astype(BF16), w_o[l].astype(BF16), row(ln2_g[l]), row(ln2_b[l]))
        h = _mix_ffn_ln(h, yr.reshape(b * s, rd), om.reshape(b * s, -1),
                        mix_wts + ffn_wts(ffn2_w1, ffn2_w3, ffn2_w2, ln3_g, ln3_b),
                        alpha=alpha, tm=tm_ffn)
    return h.reshape(b, s, d)
```

```python
import functools

import jax
import jax.numpy as jnp
from jax import lax
from jax.experimental import pallas as pl
from jax.experimental.pallas import tpu as pltpu

F32 = jnp.float32
BF16 = jnp.bfloat16

LANE = 128
SUBLANE = 8
VMEM_LIMIT_BYTES = 56 * 1024 * 1024

HEAD = 64
ROPE = 32
QK_HEAD = HEAD + ROPE
CHUNK = 64
DECAY_LORA = 64
ICLR_LORA = 64
GATE_LORA = 160
Q_LORA = 384
KV_LORA = 256
GN_EPS = 64e-5
LN_EPS = 1e-5
RMS_EPS = 1e-6
ROPE_THETA = 10000.0
LOG2E = 1.4426950408889634

MASK_LANE0 = HEAD + ROPE
MASK_NEG = -1e30

RWKV_CHUNK = 128
RWKV_LEVELS = 7


def _dot(a, b):
    return jnp.dot(a, b, preferred_element_type=F32)


def _dot_nt(a, b):
    return lax.dot_general(a, b, (((1,), (1,)), ((), ())), preferred_element_type=F32)


def _dot_tn(a, b):
    return lax.dot_general(a, b, (((0,), (0,)), ((), ())), preferred_element_type=F32)


def _split2(x):
    hi = x.astype(BF16)
    lo = (x - hi.astype(F32)).astype(BF16)
    return hi, lo


def _split3(x):
    hi = x.astype(BF16)
    r1 = x - hi.astype(F32)
    mid = r1.astype(BF16)
    lo = (r1 - mid.astype(F32)).astype(BF16)
    return hi, mid, lo


def _layer_norm(z, g, b):
    mu = jnp.mean(z, axis=-1, keepdims=True)
    zc = z - mu
    var = jnp.mean(zc * zc, axis=-1, keepdims=True)
    return zc * lax.rsqrt(var + LN_EPS) * g + b


def _rms_norm(z, g):
    return z * lax.rsqrt(jnp.mean(z * z, axis=-1, keepdims=True) + RMS_EPS) * g


def _softplus(u):
    return jnp.maximum(u, 0.0) + jnp.log(1.0 + jnp.exp(-jnp.abs(u)))


def _const_spec(shape):
    nd = len(shape)
    return pl.BlockSpec(shape, lambda *_: (0,) * nd, pipeline_mode=pl.Buffered(1))


def _swiglu_ln(x, w1_ref, w3_ref, w2_ref, g_ref, b_ref, alpha):
    xb = x.astype(BF16)
    h1 = _dot(xb, w1_ref[...])
    h3 = _dot(xb, w3_ref[...])
    act = (h1 * jax.nn.sigmoid(h1) * h3).astype(BF16)
    y = _dot(act, w2_ref[...])
    return _layer_norm(alpha * x + 0.5 * y, g_ref[...], b_ref[...])


def _ffn_ln_kernel(x_ref, w1_ref, w3_ref, w2_ref, g_ref, b_ref, o_ref, *, alpha):
    o_ref[...] = _swiglu_ln(x_ref[...], w1_ref, w3_ref, w2_ref, g_ref, b_ref, alpha)


def _ffn_ln(x2, w1, w3, w2, g, b, *, alpha, tm):
    t, d = x2.shape
    f = w1.shape[1]
    return pl.pallas_call(
        functools.partial(_ffn_ln_kernel, alpha=alpha),
        grid=(t // tm,),
        in_specs=[
            pl.BlockSpec((tm, d), lambda i: (i, 0)),
            _const_spec((d, f)), _const_spec((d, f)), _const_spec((f, d)),
            _const_spec((1, d)), _const_spec((1, d)),
        ],
        out_specs=pl.BlockSpec((tm, d), lambda i: (i, 0)),
        out_shape=jax.ShapeDtypeStruct((t, d), F32),
        compiler_params=pltpu.CompilerParams(
            dimension_semantics=("parallel",), vmem_limit_bytes=VMEM_LIMIT_BYTES),
        name="ffn_ln",
    )(x2, w1, w3, w2, g, b)


def _segsum64(x, ones2):
    outs = []
    for c in range(x.shape[1] // LANE):
        hi, lo = _split2(x[:, c * LANE:(c + 1) * LANE])
        outs.append(_dot(jnp.concatenate([hi, lo], axis=1), ones2))
    return outs[0] if len(outs) == 1 else jnp.concatenate(outs, axis=1)


def _rope(x, cos_t, sin_a, sin_b):
    return (x * cos_t + pltpu.roll(x, 16, axis=1) * sin_a
            + pltpu.roll(x, LANE - 16, axis=1) * sin_b)


def _proj_kernel(h_ref, cos_ref, sa_ref, sb_ref, km_ref, win_ref, mu_ref, wda_ref, w0_ref, a0_ref,
                 wg_ref, kk_ref, ka_ref, ones2_ref, qg_ref, wq_ref, kvg_ref, wkv_ref, vone_ref,
                 r_o, ld_o, k_o, v_o, kk_o, a_o, g_o, q_o, kx_o, vx_o, carry_ref,
                 *, rd, n_shift, q_scale):
    s = pl.program_id(1)
    tm = h_ref.shape[1]
    hb = h_ref[0].astype(BF16)
    proj = _dot(hb, win_ref[...])

    ps = proj[:, :n_shift]
    prev = pltpu.roll(ps, 1, axis=0)
    first = jnp.where(s == 0, 0.0, carry_ref[SUBLANE - 1:SUBLANE, :])
    rowid = lax.broadcasted_iota(jnp.int32, (tm, 1), 0)
    prev = jnp.where(rowid == 0, first, prev)
    carry_ref[...] = ps[tm - SUBLANE:, :]
    ps = ps + mu_ref[...] * (prev - ps)

    r = ps[:, 0:rd]
    k = ps[:, rd:2 * rd]
    v = ps[:, 2 * rd:3 * rd]
    wa = ps[:, 3 * rd:3 * rd + LANE]
    gd = ps[:, 3 * rd + LANE:n_shift]

    lane = lax.broadcasted_iota(jnp.int32, (1, LANE), 1)
    t_in = jnp.where(lane < DECAY_LORA, jnp.tanh(wa), wa).astype(BF16)
    za = _dot(t_in, wda_ref[...])
    z = w0_ref[...] + za[:, :rd]
    w = -_softplus(-z) - 0.5
    ld_o[0] = -jnp.exp(w)
    a = jax.nn.sigmoid(a0_ref[...] + za[:, rd:])
    g_o[0] = _dot(jax.nn.sigmoid(gd).astype(BF16), wg_ref[...])

    kkr = k * kk_ref[...]
    ss = _segsum64(kkr * kkr, ones2_ref[...])
    kk_o[0] = kkr * lax.rsqrt(jnp.maximum(ss, 1e-24))
    k_o[0] = k * (1.0 + (a - 1.0) * ka_ref[...])
    r_o[0] = r
    v_o[0] = v
    a_o[0] = a

    pm = proj[:, n_shift:]
    q_lat = pm[:, :Q_LORA]
    kv_lat = pm[:, Q_LORA:Q_LORA + KV_LORA]
    kp = pm[:, Q_LORA + KV_LORA:]
    cos_t, sin_a, sin_b = cos_ref[...], sa_ref[...], sb_ref[...]
    q = _dot(_rms_norm(q_lat, qg_ref[...]).astype(BF16), wq_ref[...])
    kv = _dot(_rms_norm(kv_lat, kvg_ref[...]).astype(BF16), wkv_ref[...])
    kpr = _rope(kp, cos_t, sin_a, sin_b) + km_ref[...]
    nh = q.shape[1] // LANE
    for h in range(nh):
        sl = slice(h * LANE, (h + 1) * LANE)
        q_o[0, :, sl] = (_rope(q[:, sl], cos_t, sin_a, sin_b) * q_scale).astype(BF16)
        kx_o[0, :, sl] = (kv[:, sl] + kpr).astype(BF16)
    vx_o[0] = (kv[:, nh * LANE:] + vone_ref[...]).astype(BF16)


def _proj(h3, tabs, wts, *, tm):
    b, s, d = h3.shape
    (win, mu, wda, w0, a0, wg, kkw, kaw, ones2, qg, wq, kvg, wkv, vone) = wts
    rd = w0.shape[1]
    n_shift = mu.shape[1]
    hq = wq.shape[1]
    tok = lambda n: pl.BlockSpec((1, tm, n), lambda i, j: (i, j, 0))
    tab = pl.BlockSpec((tm, LANE), lambda i, j: (j, 0))
    f32o = jax.ShapeDtypeStruct((b, s, rd), F32)
    bfo = jax.ShapeDtypeStruct((b, s, hq), BF16)
    return pl.pallas_call(
        functools.partial(_proj_kernel, rd=rd, n_shift=n_shift,
                          q_scale=float(QK_HEAD ** -0.5 * LOG2E)),
        grid=(b, s // tm),
        in_specs=[tok(d), tab, tab, tab, tab] + [_const_spec(w.shape) for w in wts],
        out_specs=[tok(rd)] * 7 + [tok(hq)] * 3,
        out_shape=[f32o] * 7 + [bfo] * 3,
        scratch_shapes=[pltpu.VMEM((SUBLANE, n_shift), F32)],
        compiler_params=pltpu.CompilerParams(
            dimension_semantics=("parallel", "arbitrary"), vmem_limit_bytes=VMEM_LIMIT_BYTES),
        name="proj",
    )(h3, *tabs, *wts)


def _wide_dot(a_w, b_w):
    c = a_w.shape[0]
    z = jnp.zeros((c, c), b_w.dtype)
    rhs = jnp.concatenate([jnp.concatenate([b_w[:, :c], z], axis=1),
                           jnp.concatenate([z, b_w[:, c:]], axis=1)], axis=0)
    return _dot(a_w, rhs)


def _rows(x, n):
    m = x.shape[0] // n
    return [x[i * m:(i + 1) * m] for i in range(n)]


def _segsum64_stacked(xs, ones2):
    parts = [jnp.concatenate(_split2(x), axis=1) for x in xs]
    return _rows(_dot(jnp.concatenate(parts, axis=0), ones2), len(xs))


def _rwkv_chunks(xs, s_prevs, cst):
    (tri, eye_w, lvl_ref, strict_w, incl_w, bd, ones2, m0, m1) = cst
    c = RWKV_CHUNK
    half = c // 2
    n = len(xs)
    each = lambda f, *ls: [f(*a) for a in zip(*ls)]
    rs, lds, ks, vs, kks, as_, gs, rks, gngs, gnbs = (list(t) for t in zip(*xs))

    ld3 = jnp.concatenate([jnp.concatenate(_split3(ld), axis=1) for ld in lds], axis=1)
    cl3 = _dot(tri, ld3)
    cls = [cl3[:, (3 * i) * LANE:(3 * i + 1) * LANE] + cl3[:, (3 * i + 1) * LANE:(3 * i + 2) * LANE]
           + cl3[:, (3 * i + 2) * LANE:(3 * i + 3) * LANE] for i in range(n)]
    refs = [cl[half - 1:half, :] for cl in cls]
    es = each(lambda cl, ref: jnp.exp(cl - ref), cls, refs)
    einvs = each(lambda cl, ref: jnp.exp(ref - cl), cls, refs)
    rts = each(lambda r, e: r * e, rs, es)
    ats = each(lambda kk, e, ld: -(kk * e) * jnp.exp(-ld), kks, es, lds)
    bts = each(lambda kk, a, ei: kk * a * ei, kks, as_, einvs)
    kts = each(lambda k, ei: k * ei, ks, einvs)

    lhs4s = each(lambda at, rt: jnp.concatenate([at * m0, at * m1, rt * m0, rt * m1],
                                                axis=0).astype(BF16), ats, rts)
    rhs2s = each(lambda bt, kt: jnp.concatenate([bt, kt], axis=0).astype(BF16), bts, kts)
    gms = each(_dot_nt, lhs4s, rhs2s)
    pair = lambda gm, r0, c0: jnp.concatenate(
        [gm[r0:r0 + c, c0:c0 + c], gm[r0 + c:r0 + 2 * c, c0:c0 + c]], axis=1)
    lab_ws = [jnp.where(strict_w, pair(gm, 0, 0), 0.0) for gm in gms]
    lak_ws = [jnp.where(strict_w, pair(gm, 0, c), 0.0) for gm in gms]
    arb_ws = [jnp.where(incl_w, pair(gm, 2 * c, 0), 0.0) for gm in gms]
    ark_ws = [jnp.where(incl_w, pair(gm, 2 * c, c), 0.0) for gm in gms]

    top = lax.broadcasted_iota(jnp.int32, (half, 2 * c), 1) % c < half
    quad = (lax.broadcasted_iota(jnp.int32, (2 * c, 2 * c), 0) // half
            == lax.broadcasted_iota(jnp.int32, (2 * c, 2 * c), 1) // half)
    pack = lambda w: jnp.where(top, w[:half], w[half:])
    quad_dot = lambda a, b: _dot(a, jnp.where(quad, jnp.concatenate([b] * 4, axis=0), 0))
    lvl_ps = [pack(lvl_ref[lv]) for lv in range(RWKV_LEVELS - 1)]
    lab_ps = [pack(lab) for lab in lab_ws]
    eye_p = pack(eye_w)
    t_ps = [eye_p + lab * lvl_ps[0] for lab in lab_ps]
    for lv in range(1, RWKV_LEVELS - 1):
        lcs = [(lab * lvl_ps[lv]).astype(BF16) for lab in lab_ps]
        tbs = [t.astype(BF16) for t in t_ps]
        xps = each(quad_dot, lcs, tbs)
        tx = each(lambda tb, xp: quad_dot(tb, xp.astype(BF16)), tbs, xps)
        t_ps = each(lambda t, d: t + d, t_ps, tx)
    t_ws = [jnp.concatenate([jnp.where(top, t, 0.0), jnp.where(top, 0.0, t)], axis=0) for t in t_ps]
    lcs = [(lab * lvl_ref[RWKV_LEVELS - 1]).astype(BF16) for lab in lab_ws]
    tbs = [t.astype(BF16) for t in t_ws]
    xws = each(_wide_dot, lcs, tbs)
    tx = each(lambda tb, xw: _wide_dot(tb, xw.astype(BF16)), tbs, xws)
    t_ws = each(lambda t, d: t + d, t_ws, tx)

    v01s = [jnp.concatenate([v * m0, v * m1], axis=0).astype(BF16) for v in vs]
    xvs = each(lambda lak, v01: _dot(lak.astype(BF16), v01), lak_ws, v01s)
    rhss = each(lambda at, xv: jnp.concatenate(
        [jnp.concatenate([at * m0, xv * m0], axis=1),
         jnp.concatenate([at * m1, xv * m1], axis=1)], axis=0).astype(BF16), ats, xvs)
    wus = each(lambda t, rhs: _dot(t.astype(BF16), rhs), t_ws, rhss)
    sps = each(lambda s, ref: s * jnp.exp(ref), s_prevs, refs)
    wss = each(lambda wu, rt, sp: _dot_nt(
        jnp.concatenate([wu[:, :LANE], rt], axis=0).astype(BF16), sp.astype(BF16)), wus, rts, sps)
    us = each(lambda ws, wu: ws[:c] + wu[:, LANE:], wss, wus)
    ys = each(lambda ws, arb, ark, u, v01: ws[c:] + _dot(
        jnp.concatenate([arb, ark], axis=1).astype(BF16),
        jnp.concatenate([(u * m0).astype(BF16), (u * m1).astype(BF16), v01], axis=0)),
        wss, arb_ws, ark_ws, us, v01s)
    upds = each(lambda u, v, rhs2: _dot_tn(jnp.concatenate([u, v], axis=0).astype(BF16), rhs2),
                us, vs, rhs2s)
    s_news = each(lambda sp, upd, e: (sp + jnp.where(bd, upd, 0.0)) * e[c - 1:c, :],
                  sps, upds, es)

    mus = _segsum64_stacked(ys, ones2)
    ycs = each(lambda y, mu: y - mu * (1.0 / HEAD), ys, mus)
    vars_ = _segsum64_stacked([yc * yc for yc in ycs], ones2)
    bsums = _segsum64_stacked(each(lambda r, k, rk: r * k * rk, rs, ks, rks), ones2)
    outs = each(lambda yc, var, bs, v, g, gng, gnb: (
        yc * lax.rsqrt(var * (1.0 / HEAD) + GN_EPS) * gng + gnb + bs * v) * g,
        ycs, vars_, bsums, vs, gs, gngs, gnbs)
    return outs, s_news


def _rwkv_kernel(r_ref, ld_ref, k_ref, v_ref, kk_ref, a_ref, g_ref, rk_ref, gng_ref, gnb_ref,
                 tri_ref, lvl_ref, ones2_ref, o_ref, s_ref, *, group):
    c = RWKV_CHUNK

    @pl.when(pl.program_id(1) == 0)
    def _():
        s_ref[...] = jnp.zeros_like(s_ref)

    row = lax.broadcasted_iota(jnp.int32, (c, 2 * c), 0)
    colw = lax.broadcasted_iota(jnp.int32, (c, 2 * c), 1) & (c - 1)
    strict_w = row > colw
    incl_w = row >= colw
    eye_w = (row == colw).astype(F32)
    r2 = lax.broadcasted_iota(jnp.int32, (LANE, LANE), 0)
    c2 = lax.broadcasted_iota(jnp.int32, (LANE, LANE), 1)
    bd = (r2 >= HEAD) == (c2 >= HEAD)
    lane = lax.broadcasted_iota(jnp.int32, (1, LANE), 1)
    m0 = (lane < HEAD).astype(F32)
    m1 = 1.0 - m0
    cst = (tri_ref[...], eye_w, lvl_ref, strict_w, incl_w, bd, ones2_ref[...], m0, m1)
    in_refs = (r_ref, ld_ref, k_ref, v_ref, kk_ref, a_ref, g_ref)
    par_refs = (rk_ref, gng_ref, gnb_ref)
    pairs = [slice(p * LANE, (p + 1) * LANE) for p in range(r_ref.shape[2] // LANE)]

    def body(i, carry):
        probs = [(p, sl, i * group + j) for p, sl in enumerate(pairs) for j in range(group)]
        outs, s_news = _rwkv_chunks(
            [tuple(ref[b, :, sl] for ref in in_refs) + tuple(ref[:, sl] for ref in par_refs)
             for _, sl, b in probs],
            [s_ref[p, b] for p, _, b in probs], cst)
        for (p, sl, b), out, s_new in zip(probs, outs, s_news):
            s_ref[p, b] = s_new
            o_ref[b, :, sl] = out.astype(o_ref.dtype)
        return carry

    lax.fori_loop(0, r_ref.shape[0] // group, body, 0)


def _rwkv(acts, rk, gng, gnb, consts, *, group, pairs):
    b, s, rd = acts[0].shape
    c = RWKV_CHUNK
    tri, lvl, ones2 = consts
    act = pl.BlockSpec((b, c, pairs * LANE), lambda p, t: (0, t, p))
    par = pl.BlockSpec((1, pairs * LANE), lambda p, t: (0, p))
    return pl.pallas_call(
        functools.partial(_rwkv_kernel, group=group),
        grid=(rd // (pairs * LANE), s // c),
        in_specs=[act] * 7 + [par] * 3 + [_const_spec(tri.shape), _const_spec(lvl.shape),
                                          _const_spec(ones2.shape)],
        out_specs=act,
        out_shape=jax.ShapeDtypeStruct((b, s, rd), BF16),
        scratch_shapes=[pltpu.VMEM((pairs, b, LANE, LANE), F32)],
        compiler_params=pltpu.CompilerParams(
            dimension_semantics=("parallel", "arbitrary"), vmem_limit_bytes=VMEM_LIMIT_BYTES),
        name="rwkv",
    )(*acts, rk, gng, gnb, tri, lvl, ones2)


def _attn_kernel(q_ref, k_ref, v_ref, qm_ref, o_ref, s_scr, mx_scr, acc_scr, m_scr,
                 alpha_scr, *, tq, tk, rb):
    nq = q_ref.shape[1] // tq
    ratio = tk // tq
    n_steps = sum(i // ratio + 1 for i in range(nq))
    lane = lax.broadcasted_iota(jnp.int32, (1, LANE), 1)
    heads = range(2)
    hs = lambda h: slice(h * LANE, (h + 1) * LANE)

    def advance(ij):
        i, j = ij
        last = j == i // ratio
        ni = jnp.where(last, jnp.minimum(i + 1, nq - 1), i)
        nj = jnp.where(last, jnp.where(i == nq - 1, j, 0), j + 1)
        return ni, nj

    def lane_block_max(sc):
        mx = sc[:, :LANE]
        for c0 in range(LANE, tk, LANE):
            mx = jnp.maximum(mx, sc[:, c0:c0 + LANE])
        return mx

    def scores(ij, slot, heads=heads):
        i, j = ij
        q0 = pl.multiple_of(i * tq, tq)
        k0 = pl.multiple_of(j * tk, tk)
        qm = qm_ref[jnp.where(j == i // ratio, 1 + i % ratio, 0)]
        for h in heads:
            sc = _dot_nt(q_ref[0, pl.ds(q0, tq), hs(h)] + qm, k_ref[0, pl.ds(k0, tk), hs(h)])
            s_scr[slot, h] = sc
            mx_scr[slot, h] = lane_block_max(sc)

    def softmax(ij, slot, heads=heads):
        j = ij[1]
        cols = [pl.ds(c0, LANE) for c0 in range(0, tk, LANE)]
        ps = []
        for h in heads:
            blocks = []
            for r0 in range(0, tq, rb):
                rows = pl.ds(r0, rb)
                mx = jnp.broadcast_to(
                    jnp.max(mx_scr[slot, h, rows, :], axis=1, keepdims=True), (rb, LANE))
                m_old = jnp.where(j == 0, -jnp.inf, m_scr[h, rows, :])
                m_new = jnp.maximum(m_old, mx)
                blocks.append(jnp.concatenate(
                    [jnp.exp2(s_scr[slot, h, rows, cs] - m_new).astype(BF16) for cs in cols], axis=1))
                m_scr[h, rows, :] = m_new
                alpha_scr[h, rows, :] = jnp.exp2(m_old - m_new)
            ps.append(jnp.concatenate(blocks, axis=0))
        return ps

    def values(ij, ps, heads=heads):
        k0 = pl.multiple_of(ij[1] * tk, tk)
        for h, p in zip(heads, ps):
            acc_scr[h] = (alpha_scr[h] * acc_scr[h]
                          + _dot(p, v_ref[0, pl.ds(k0, tk), hs(h)]))

    def finalize(i):
        q0 = pl.multiple_of(i * tq, tq)
        a0, a1 = acc_scr[0], acc_scr[1]
        out = jnp.where(lane < HEAD, a0 / a0[:, HEAD:HEAD + 1], a1 / a1[:, 0:1])
        o_ref[0, pl.ds(q0, tq), :] = out.astype(o_ref.dtype)

    acc_scr[...] = jnp.zeros_like(acc_scr)
    m_scr[...] = jnp.full_like(m_scr, -jnp.inf)
    zero = jnp.int32(0)
    ij0 = (zero, zero)
    scores(ij0, 0)

    def step(carry, slot):
        ij_a, ij_b = carry
        for h in heads:
            scores(ij_b, 1 - slot, (h,))
            values(ij_a, softmax(ij_a, slot, (h,)), (h,))

        @pl.when(ij_a[1] == ij_a[0] // ratio)
        def _():
            finalize(ij_a[0])

        return ij_b, advance(ij_b)

    carry = (ij0, advance(ij0))
    first = n_steps % 2
    if first:
        carry = step(carry, 0)
    lax.fori_loop(0, n_steps // 2,
                  lambda _, c: step(step(c, first), 1 - first), carry)


def _attn(q, k, v, qm, *, tq, tk):
    b, s, hq = q.shape
    pairs = hq // (2 * LANE)
    blk = pl.BlockSpec((1, s, 2 * LANE), lambda i, p: (i, 0, p))
    return pl.pallas_call(
        functools.partial(_attn_kernel, tq=tq, tk=tk, rb=min(64, tq)),
        grid=(b, pairs),
        in_specs=[blk, blk, blk, _const_spec(qm.shape)],
        out_specs=pl.BlockSpec((1, s, LANE), lambda i, p: (i, 0, p)),
        out_shape=jax.ShapeDtypeStruct((b, s, pairs * LANE), BF16),
        scratch_shapes=[pltpu.VMEM((2, 2, tq, tk), F32), pltpu.VMEM((2, 2, tq, LANE), F32),
                        pltpu.VMEM((2, tq, LANE), F32), pltpu.VMEM((2, tq, LANE), F32),
                        pltpu.VMEM((2, tq, LANE), F32)],
        compiler_params=pltpu.CompilerParams(
            dimension_semantics=("parallel", "parallel"), vmem_limit_bytes=VMEM_LIMIT_BYTES),
        name="attn",
    )(q, k, v, qm)


def _mix_ffn_ln_kernel(h_ref, yr_ref, om_ref, wgate_ref, wur_ref, wum_ref, wo_ref, g2_ref, b2_ref,
                       w1_ref, w3_ref, w2_ref, g3_ref, b3_ref, o_ref, *, alpha):
    h = h_ref[...]
    d = h.shape[1]
    gates = jax.nn.sigmoid(_dot(h.astype(BF16), wgate_ref[...]))
    y_r = _dot(yr_ref[...], wur_ref[...])
    y_m = _dot(om_ref[...], wum_ref[...])
    mixin = (gates[:, :d] * y_r + gates[:, d:] * y_m).astype(BF16)
    mix = _dot(mixin, wo_ref[...])
    h2 = _layer_norm(alpha * h + mix, g2_ref[...], b2_ref[...])
    o_ref[...] = _swiglu_ln(h2, w1_ref, w3_ref, w2_ref, g3_ref, b3_ref, alpha)


def _mix_ffn_ln(h2, yr, om, wts, *, alpha, tm):
    t, d = h2.shape
    tok = lambda n: pl.BlockSpec((tm, n), lambda i: (i, 0))
    return pl.pallas_call(
        functools.partial(_mix_ffn_ln_kernel, alpha=alpha),
        grid=(t // tm,),
        in_specs=[tok(d), tok(yr.shape[1]), tok(om.shape[1])] + [_const_spec(w.shape) for w in wts],
        out_specs=tok(d),
        out_shape=jax.ShapeDtypeStruct((t, d), F32),
        compiler_params=pltpu.CompilerParams(
            dimension_semantics=("parallel",), vmem_limit_bytes=VMEM_LIMIT_BYTES),
        name="mix_ffn_ln",
    )(h2, yr, om, *wts)


def _pad_cols(w, n):
    return jnp.pad(w, ((0, 0), (0, n - w.shape[1])))


def _rope_tables(s):
    inv_freq = ROPE_THETA ** (-jnp.arange(0, ROPE, 2, dtype=F32) / ROPE)
    ang = jnp.arange(s, dtype=F32)[:, None] * inv_freq[None, :]
    cos, sin = jnp.cos(ang), jnp.sin(ang)
    half = ROPE // 2
    z = lambda n: jnp.zeros((s, n), F32)
    cos_t = jnp.concatenate([jnp.ones((s, HEAD), F32), cos, cos, z(LANE - HEAD - ROPE)], axis=1)
    sin_a = jnp.concatenate([z(HEAD + half), sin, z(LANE - HEAD - ROPE)], axis=1)
    sin_b = jnp.concatenate([z(HEAD), -sin, z(LANE - HEAD - half)], axis=1)
    return cos_t, sin_a, sin_b


def _rwkv_consts():
    c = RWKV_CHUNK
    i = jnp.arange(c)[:, None]
    j = jnp.arange(c)[None, :]
    tri = (i >= j).astype(BF16)
    lv = []
    for lg in range(RWKV_LEVELS):
        m = ((i >> (lg + 1)) == (j >> (lg + 1))) & (((i >> lg) & 1) == 1) & (((j >> lg) & 1) == 0)
        lv.append(jnp.concatenate([m, m], axis=1).astype(F32))
    return tri, jnp.stack(lv)


def _ones2():
    i = jnp.arange(2 * LANE)[:, None] % LANE
    j = jnp.arange(LANE)[None, :]
    return ((i // HEAD) == (j // HEAD)).astype(BF16)


def _mask_tables(s, tq, tk):
    nc = tk // CHUNK
    assert MASK_LANE0 + nc <= LANE
    lane = jnp.arange(LANE)[None, :] - MASK_LANE0
    kc = (jnp.arange(s)[:, None] % tk) // CHUNK
    kmask = jnp.where((lane >= 0) & (lane < nc) & (kc > lane), MASK_NEG, 0.0).astype(F32)
    qm = [jnp.zeros((tq, LANE), F32)]
    for v in range(tk // tq):
        qc = (v * tq + jnp.arange(tq)[:, None]) // CHUNK
        qm.append((lane == qc).astype(F32))
    return kmask, jnp.stack(qm).astype(BF16)


def kernel(x, ffn1_w1, ffn1_w3, ffn1_w2, ln1_g, ln1_b, w_in, mu_shift, w0, w_decay_up, a0, w_iclr_up, w_gate_up, k_k, k_a, r_k, gn_g, gn_b, q_norm_g, w_q_up, kv_norm_g, w_kv_up, w_up_rwkv, w_up_mla, w_o, ln2_g, ln2_b, ffn2_w1, ffn2_w3, ffn2_w2, ln3_g, ln3_b):
    b, s, d = x.shape
    depth = ffn1_w1.shape[0]
    rd = w0.shape[1]
    nh = d // LANE
    alpha = float((2.0 * depth) ** 0.25)
    f = ffn1_w1.shape[2]
    fp = -(-f // LANE) * LANE
    tm_ffn = min(256, b * s)
    tm_proj = min(512, s)
    tq = min(512, s)
    tk = min(2 * tq, s)
    row = lambda p: p.reshape(1, -1).astype(F32)

    kmask, qm = _mask_tables(s, tq, tk)
    tabs = _rope_tables(s) + (kmask,)
    tri, lvl = _rwkv_consts()
    ones2 = _ones2()
    lane = jnp.arange(nh * LANE) % LANE
    head = jnp.arange(nh * LANE) // LANE
    vone = jnp.where(head % 2 == 0, lane == HEAD, lane == 0).astype(F32)[None, :]

    n_rkv = 3 * rd
    n_lora = n_rkv + DECAY_LORA + ICLR_LORA
    n_shift = n_lora + GATE_LORA
    n_shift_p = n_lora + 2 * LANE
    n_mla = n_shift + Q_LORA + KV_LORA

    h = x.reshape(b * s, d)
    for l in range(depth):
        ffn_wts = lambda w1, w3, w2, g, bb: (
            _pad_cols(w1[l], fp).astype(BF16), _pad_cols(w3[l], fp).astype(BF16),
            jnp.pad(w2[l], ((0, fp - f), (0, 0))).astype(BF16), row(g[l]), row(bb[l]))
        h = _ffn_ln(h, *ffn_wts(ffn1_w1, ffn1_w3, ffn1_w2, ln1_g, ln1_b), alpha=alpha,
                    tm=min(512, b * s))

        w = w_in[l]
        z = lambda n: jnp.zeros((d, n), F32)
        kpe_blk = jnp.concatenate([z(HEAD), w[:, n_mla:n_mla + ROPE], z(LANE - HEAD - ROPE)], axis=1)
        win = jnp.concatenate([w[:, :n_shift], z(n_shift_p - n_shift), w[:, n_shift:n_mla], kpe_blk],
                              axis=1).astype(BF16)
        mu = jnp.pad(mu_shift[l], (0, n_shift_p - n_shift))[None, :]
        zl = jnp.zeros((DECAY_LORA, rd), F32)
        wda = jnp.concatenate([jnp.concatenate([w_decay_up[l], zl], axis=1),
                               jnp.concatenate([zl, w_iclr_up[l]], axis=1)], axis=0).astype(BF16)
        wg = jnp.pad(w_gate_up[l], ((0, 2 * LANE - GATE_LORA), (0, 0))).astype(BF16)
        wq = jnp.pad(w_q_up[l].reshape(Q_LORA, nh, QK_HEAD),
                     ((0, 0), (0, 0), (0, LANE - QK_HEAD))).reshape(Q_LORA, nh * LANE).astype(BF16)
        wkv3 = w_kv_up[l].reshape(KV_LORA, nh, 2 * HEAD)
        zk = jnp.zeros((KV_LORA, nh, HEAD), F32)
        wk = jnp.concatenate([wkv3[:, :, :HEAD], zk], axis=2).reshape(KV_LORA, nh * LANE)
        wv_even = jnp.concatenate([wkv3[:, :, HEAD:], zk], axis=2)
        wv_odd = jnp.concatenate([zk, wkv3[:, :, HEAD:]], axis=2)
        odd = (jnp.arange(nh) % 2 == 1)[None, :, None]
        wv = jnp.where(odd, wv_odd, wv_even).reshape(KV_LORA, nh * LANE)
        wkv = jnp.concatenate([wk, wv], axis=1).astype(BF16)
        wts = (win, mu, wda, row(w0[l]), row(a0[l]), wg, row(k_k[l]), row(k_a[l]), ones2,
               row(q_norm_g[l]), wq, row(kv_norm_g[l]), wkv, vone)
        r, ld, km, v, kk, a, g, q, kx, vx = _proj(h.reshape(b, s, d), tabs, wts, tm=tm_proj)

        yr = _rwkv((r, ld, km, v, kk, a, g), row(r_k[l]), row(gn_g[l]), row(gn_b[l]),
                   (tri, lvl, ones2), group=min(8, b), pairs=2)
        om = _attn(q, kx, vx, qm, tq=tq, tk=tk)

        mix_wts = (w[:, n_mla + ROPE:].astype(BF16), w_up_rwkv[l].astype(BF16),
                   w_up_mla[l].astype(BF16), w_o[l].astype(BF16), row(ln2_g[l]), row(ln2_b[l]))
        h = _mix_ffn_ln(h, yr.reshape(b * s, rd), om.reshape(b * s, -1),
                        mix_wts + ffn_wts(ffn2_w1, ffn2_w3, ffn2_w2, ln3_g, ln3_b),
                        alpha=alpha, tm=tm_ffn)
    return h.reshape(b, s, d)
```

```python
import functools

import jax
import jax.numpy as jnp
from jax import lax
from jax.experimental import pallas as pl
from jax.experimental.pallas import tpu as pltpu

F32 = jnp.float32
BF16 = jnp.bfloat16

LANE = 128
SUBLANE = 8
VMEM_LIMIT_BYTES = 56 * 1024 * 1024

HEAD = 64
ROPE = 32
QK_HEAD = HEAD + ROPE
CHUNK = 64
DECAY_LORA = 64
ICLR_LORA = 64
GATE_LORA = 160
Q_LORA = 384
KV_LORA = 256
GN_EPS = 64e-5
LN_EPS = 1e-5
RMS_EPS = 1e-6
ROPE_THETA = 10000.0
LOG2E = 1.4426950408889634

MASK_LANE0 = HEAD + ROPE
MASK_NEG = -1e30

RWKV_CHUNK = 128
RWKV_LEVELS = 7


def _dot(a, b):
    return jnp.dot(a, b, preferred_element_type=F32)


def _dot_nt(a, b):
    return lax.dot_general(a, b, (((1,), (1,)), ((), ())), preferred_element_type=F32)


def _dot_tn(a, b):
    return lax.dot_general(a, b, (((0,), (0,)), ((), ())), preferred_element_type=F32)


def _split2(x):
    hi = x.astype(BF16)
    lo = (x - hi.astype(F32)).astype(BF16)
    return hi, lo


def _split3(x):
    hi = x.astype(BF16)
    r1 = x - hi.astype(F32)
    mid = r1.astype(BF16)
    lo = (r1 - mid.astype(F32)).astype(BF16)
    return hi, mid, lo


def _layer_norm(z, g, b):
    mu = jnp.mean(z, axis=-1, keepdims=True)
    zc = z - mu
    var = jnp.mean(zc * zc, axis=-1, keepdims=True)
    return zc * lax.rsqrt(var + LN_EPS) * g + b


def _rms_norm(z, g):
    return z * lax.rsqrt(jnp.mean(z * z, axis=-1, keepdims=True) + RMS_EPS) * g


def _softplus(u):
    return jnp.maximum(u, 0.0) + jnp.log(1.0 + jnp.exp(-jnp.abs(u)))


def _const_spec(shape):
    nd = len(shape)
    return pl.BlockSpec(shape, lambda *_: (0,) * nd, pipeline_mode=pl.Buffered(1))


def _swiglu_ln(x, w1_ref, w3_ref, w2_ref, g_ref, b_ref, alpha):
    xb = x.astype(BF16)
    h1 = _dot(xb, w1_ref[...])
    h3 = _dot(xb, w3_ref[...])
    act = (h1 * jax.nn.sigmoid(h1) * h3).astype(BF16)
    y = _dot(act, w2_ref[...])
    return _layer_norm(alpha * x + 0.5 * y, g_ref[...], b_ref[...])


def _ffn_ln_kernel(x_ref, w1_ref, w3_ref, w2_ref, g_ref, b_ref, o_ref, *, alpha):
    o_ref[...] = _swiglu_ln(x_ref[...], w1_ref, w3_ref, w2_ref, g_ref, b_ref, alpha)


def _ffn_ln(x2, w1, w3, w2, g, b, *, alpha, tm):
    t, d = x2.shape
    f = w1.shape[1]
    return pl.pallas_call(
        functools.partial(_ffn_ln_kernel, alpha=alpha),
        grid=(t // tm,),
        in_specs=[
            pl.BlockSpec((tm, d), lambda i: (i, 0)),
            _const_spec((d, f)), _const_spec((d, f)), _const_spec((f, d)),
            _const_spec((1, d)), _const_spec((1, d)),
        ],
        out_specs=pl.BlockSpec((tm, d), lambda i: (i, 0)),
        out_shape=jax.ShapeDtypeStruct((t, d), F32),
        compiler_params=pltpu.CompilerParams(
            dimension_semantics=("parallel",), vmem_limit_bytes=VMEM_LIMIT_BYTES),
        name="ffn_ln",
    )(x2, w1, w3, w2, g, b)


def _segsum64(x, ones2):
    outs = []
    for c in range(x.shape[1] // LANE):
        hi, lo = _split2(x[:, c * LANE:(c + 1) * LANE])
        outs.append(_dot(jnp.concatenate([hi, lo], axis=1), ones2))
    return outs[0] if len(outs) == 1 else jnp.concatenate(outs, axis=1)


def _rope(x, cos_t, sin_a, sin_b):
    return (x * cos_t + pltpu.roll(x, 16, axis=1) * sin_a
            + pltpu.roll(x, LANE - 16, axis=1) * sin_b)


def _proj_kernel(h_ref, cos_ref, sa_ref, sb_ref, km_ref, win_ref, mu_ref, wda_ref, w0_ref, a0_ref,
                 wg_ref, kk_ref, ka_ref, ones2_ref, qg_ref, wq_ref, kvg_ref, wkv_ref, vone_ref,
                 r_o, ld_o, k_o, v_o, kk_o, a_o, g_o, q_o, kx_o, vx_o, carry_ref,
                 *, rd, n_shift, q_scale):
    s = pl.program_id(1)
    tm = h_ref.shape[1]
    hb = h_ref[0].astype(BF16)
    proj = _dot(hb, win_ref[...])

    ps = proj[:, :n_shift]
    prev = pltpu.roll(ps, 1, axis=0)
    first = jnp.where(s == 0, 0.0, carry_ref[SUBLANE - 1:SUBLANE, :])
    rowid = lax.broadcasted_iota(jnp.int32, (tm, 1), 0)
    prev = jnp.where(rowid == 0, first, prev)
    carry_ref[...] = ps[tm - SUBLANE:, :]
    ps = ps + mu_ref[...] * (prev - ps)

    r = ps[:, 0:rd]
    k = ps[:, rd:2 * rd]
    v = ps[:, 2 * rd:3 * rd]
    wa = ps[:, 3 * rd:3 * rd + LANE]
    gd = ps[:, 3 * rd + LANE:n_shift]

    lane = lax.broadcasted_iota(jnp.int32, (1, LANE), 1)
    t_in = jnp.where(lane < DECAY_LORA, jnp.tanh(wa), wa).astype(BF16)
    za = _dot(t_in, wda_ref[...])
    z = w0_ref[...] + za[:, :rd]
    w = -_softplus(-z) - 0.5
    ld_o[0] = -jnp.exp(w)
    a = jax.nn.sigmoid(a0_ref[...] + za[:, rd:])
    g_o[0] = _dot(jax.nn.sigmoid(gd).astype(BF16), wg_ref[...])

    kkr = k * kk_ref[...]
    ss = _segsum64(kkr * kkr, ones2_ref[...])
    kk_o[0] = kkr * lax.rsqrt(jnp.maximum(ss, 1e-24))
    k_o[0] = k * (1.0 + (a - 1.0) * ka_ref[...])
    r_o[0] = r
    v_o[0] = v
    a_o[0] = a

    pm = proj[:, n_shift:]
    q_lat = pm[:, :Q_LORA]
    kv_lat = pm[:, Q_LORA:Q_LORA + KV_LORA]
    kp = pm[:, Q_LORA + KV_LORA:]
    cos_t, sin_a, sin_b = cos_ref[...], sa_ref[...], sb_ref[...]
    q = _dot(_rms_norm(q_lat, qg_ref[...]).astype(BF16), wq_ref[...])
    kv = _dot(_rms_norm(kv_lat, kvg_ref[...]).astype(BF16), wkv_ref[...])
    kpr = _rope(kp, cos_t, sin_a, sin_b) + km_ref[...]
    nh = q.shape[1] // LANE
    for h in range(nh):
        sl = slice(h * LANE, (h + 1) * LANE)
        q_o[0, :, sl] = (_rope(q[:, sl], cos_t, sin_a, sin_b) * q_scale).astype(BF16)
        kx_o[0, :, sl] = (kv[:, sl] + kpr).astype(BF16)
    vx_o[0] = (kv[:, nh * LANE:] + vone_ref[...]).astype(BF16)


def _proj(h3, tabs, wts, *, tm):
    b, s, d = h3.shape
    (win, mu, wda, w0, a0, wg, kkw, kaw, ones2, qg, wq, kvg, wkv, vone) = wts
    rd = w0.shape[1]
    n_shift = mu.shape[1]
    hq = wq.shape[1]
    tok = lambda n: pl.BlockSpec((1, tm, n), lambda i, j: (i, j, 0))
    tab = pl.BlockSpec((tm, LANE), lambda i, j: (j, 0))
    f32o = jax.ShapeDtypeStruct((b, s, rd), F32)
    bfo = jax.ShapeDtypeStruct((b, s, hq), BF16)
    return pl.pallas_call(
        functools.partial(_proj_kernel, rd=rd, n_shift=n_shift,
                          q_scale=float(QK_HEAD ** -0.5 * LOG2E)),
        grid=(b, s // tm),
        in_specs=[tok(d), tab, tab, tab, tab] + [_const_spec(w.shape) for w in wts],
        out_specs=[tok(rd)] * 7 + [tok(hq)] * 3,
        out_shape=[f32o] * 7 + [bfo] * 3,
        scratch_shapes=[pltpu.VMEM((SUBLANE, n_shift), F32)],
        compiler_params=pltpu.CompilerParams(
            dimension_semantics=("parallel", "arbitrary"), vmem_limit_bytes=VMEM_LIMIT_BYTES),
        name="proj",
    )(h3, *tabs, *wts)


def _wide_dot(a_w, b_w):
    c = a_w.shape[0]
    z = jnp.zeros((c, c), b_w.dtype)
    rhs = jnp.concatenate([jnp.concatenate([b_w[:, :c], z], axis=1),
                           jnp.concatenate([z, b_w[:, c:]], axis=1)], axis=0)
    return _dot(a_w, rhs)


def _rows(x, n):
    m = x.shape[0] // n
    return [x[i * m:(i + 1) * m] for i in range(n)]


def _segsum64_stacked(xs, ones2):
    parts = [jnp.concatenate(_split2(x), axis=1) for x in xs]
    return _rows(_dot(jnp.concatenate(parts, axis=0), ones2), len(xs))


def _rwkv_chunks(xs, s_prevs, cst):
    (tri, eye_w, lvl_ref, strict_w, incl_w, bd, ones2, m0, m1) = cst
    c = RWKV_CHUNK
    half = c // 2
    n = len(xs)
    each = lambda f, *ls: [f(*a) for a in zip(*ls)]
    rs, lds, ks, vs, kks, as_, gs, rks, gngs, gnbs = (list(t) for t in zip(*xs))

    ld3 = jnp.concatenate([jnp.concatenate(_split3(ld), axis=1) for ld in lds], axis=1)
    cl3 = _dot(tri, ld3)
    cls = [cl3[:, (3 * i) * LANE:(3 * i + 1) * LANE] + cl3[:, (3 * i + 1) * LANE:(3 * i + 2) * LANE]
           + cl3[:, (3 * i + 2) * LANE:(3 * i + 3) * LANE] for i in range(n)]
    refs = [cl[half - 1:half, :] for cl in cls]
    es = each(lambda cl, ref: jnp.exp(cl - ref), cls, refs)
    einvs = each(lambda cl, ref: jnp.exp(ref - cl), cls, refs)
    rts = each(lambda r, e: r * e, rs, es)
    ats = each(lambda kk, e, ld: -(kk * e) * jnp.exp(-ld), kks, es, lds)
    bts = each(lambda kk, a, ei: kk * a * ei, kks, as_, einvs)
    kts = each(lambda k, ei: k * ei, ks, einvs)

    lhs4s = each(lambda at, rt: jnp.concatenate([at * m0, at * m1, rt * m0, rt * m1],
                                                axis=0).astype(BF16), ats, rts)
    rhs2s = each(lambda bt, kt: jnp.concatenate([bt, kt], axis=0).astype(BF16), bts, kts)
    gms = each(_dot_nt, lhs4s, rhs2s)
    pair = lambda gm, r0, c0: jnp.concatenate(
        [gm[r0:r0 + c, c0:c0 + c], gm[r0 + c:r0 + 2 * c, c0:c0 + c]], axis=1)
    lab_ws = [jnp.where(strict_w, pair(gm, 0, 0), 0.0) for gm in gms]
    lak_ws = [jnp.where(strict_w, pair(gm, 0, c), 0.0) for gm in gms]
    arb_ws = [jnp.where(incl_w, pair(gm, 2 * c, 0), 0.0) for gm in gms]
    ark_ws = [jnp.where(incl_w, pair(gm, 2 * c, c), 0.0) for gm in gms]

    top = lax.broadcasted_iota(jnp.int32, (half, 2 * c), 1) % c < half
    quad = (lax.broadcasted_iota(jnp.int32, (2 * c, 2 * c), 0) // half
            == lax.broadcasted_iota(jnp.int32, (2 * c, 2 * c), 1) // half)
    pack = lambda w: jnp.where(top, w[:half], w[half:])
    quad_dot = lambda a, b: _dot(a, jnp.where(quad, jnp.concatenate([b] * 4, axis=0), 0))
    lvl_ps = [pack(lvl_ref[lv]) for lv in range(RWKV_LEVELS - 1)]
    lab_ps = [pack(lab) for lab in lab_ws]
    eye_p = pack(eye_w)
    t_ps = [eye_p + lab * lvl_ps[0] for lab in lab_ps]
    for lv in range(1, RWKV_LEVELS - 1):
        lcs = [(lab * lvl_ps[lv]).astype(BF16) for lab in lab_ps]
        tbs = [t.astype(BF16) for t in t_ps]
        xps = each(quad_dot, lcs, tbs)
        tx = each(lambda tb, xp: quad_dot(tb, xp.astype(BF16)), tbs, xps)
        t_ps = each(lambda t, d: t + d, t_ps, tx)
    t_ws = [jnp.concatenate([jnp.where(top, t, 0.0), jnp.where(top, 0.0, t)], axis=0) for t in t_ps]
    lcs = [(lab * lvl_ref[RWKV_LEVELS - 1]).astype(BF16) for lab in lab_ws]
    tbs = [t.astype(BF16) for t in t_ws]
    xws = each(_wide_dot, lcs, tbs)
    tx = each(lambda tb, xw: _wide_dot(tb, xw.astype(BF16)), tbs, xws)
    t_ws = each(lambda t, d: t + d, t_ws, tx)

    v01s = [jnp.concatenate([v * m0, v * m1], axis=0).astype(BF16) for v in vs]
    xvs = each(lambda lak, v01: _dot(lak.astype(BF16), v01), lak_ws, v01s)
    rhss = each(lambda at, xv: jnp.concatenate(
        [jnp.concatenate([at * m0, xv * m0], axis=1),
         jnp.concatenate([at * m1, xv * m1], axis=1)], axis=0).astype(BF16), ats, xvs)
    wus = each(lambda t, rhs: _dot(t.astype(BF16), rhs), t_ws, rhss)
    sps = each(lambda s, ref: s * jnp.exp(ref), s_prevs, refs)
    wss = each(lambda wu, rt, sp: _dot_nt(
        jnp.concatenate([wu[:, :LANE], rt], axis=0).astype(BF16), sp.astype(BF16)), wus, rts, sps)
    us = each(lambda ws, wu: ws[:c] + wu[:, LANE:], wss, wus)
    ys = each(lambda ws, arb, ark, u, v01: ws[c:] + _dot(
        jnp.concatenate([arb, ark], axis=1).astype(BF16),
        jnp.concatenate([(u * m0).astype(BF16), (u * m1).astype(BF16), v01], axis=0)),
        wss, arb_ws, ark_ws, us, v01s)
    upds = each(lambda u, v, rhs2: _dot_tn(jnp.concatenate([u, v], axis=0).astype(BF16), rhs2),
                us, vs, rhs2s)
    s_news = each(lambda sp, upd, e: (sp + jnp.where(bd, upd, 0.0)) * e[c - 1:c, :],
                  sps, upds, es)

    mus = _segsum64_stacked(ys, ones2)
    ycs = each(lambda y, mu: y - mu * (1.0 / HEAD), ys, mus)
    vars_ = _segsum64_stacked([yc * yc for yc in ycs], ones2)
    bsums = _segsum64_stacked(each(lambda r, k, rk: r * k * rk, rs, ks, rks), ones2)
    outs = each(lambda yc, var, bs, v, g, gng, gnb: (
        yc * lax.rsqrt(var * (1.0 / HEAD) + GN_EPS) * gng + gnb + bs * v) * g,
        ycs, vars_, bsums, vs, gs, gngs, gnbs)
    return outs, s_news


def _rwkv_kernel(r_ref, ld_ref, k_ref, v_ref, kk_ref, a_ref, g_ref, rk_ref, gng_ref, gnb_ref,
                 tri_ref, lvl_ref, ones2_ref, o_ref, s_ref, *, group):
    c = RWKV_CHUNK

    @pl.when(pl.program_id(1) == 0)
    def _():
        s_ref[...] = jnp.zeros_like(s_ref)

    row = lax.broadcasted_iota(jnp.int32, (c, 2 * c), 0)
    colw = lax.broadcasted_iota(jnp.int32, (c, 2 * c), 1) & (c - 1)
    strict_w = row > colw
    incl_w = row >= colw
    eye_w = (row == colw).astype(F32)
    r2 = lax.broadcasted_iota(jnp.int32, (LANE, LANE), 0)
    c2 = lax.broadcasted_iota(jnp.int32, (LANE, LANE), 1)
    bd = (r2 >= HEAD) == (c2 >= HEAD)
    lane = lax.broadcasted_iota(jnp.int32, (1, LANE), 1)
    m0 = (lane < HEAD).astype(F32)
    m1 = 1.0 - m0
    cst = (tri_ref[...], eye_w, lvl_ref, strict_w, incl_w, bd, ones2_ref[...], m0, m1)
    in_refs = (r_ref, ld_ref, k_ref, v_ref, kk_ref, a_ref, g_ref)
    par_refs = (rk_ref, gng_ref, gnb_ref)
    pairs = [slice(p * LANE, (p + 1) * LANE) for p in range(r_ref.shape[2] // LANE)]

    def body(i, carry):
        probs = [(p, sl, i * group + j) for p, sl in enumerate(pairs) for j in range(group)]
        outs, s_news = _rwkv_chunks(
            [tuple(ref[b, :, sl] for ref in in_refs) + tuple(ref[:, sl] for ref in par_refs)
             for _, sl, b in probs],
            [s_ref[p, b] for p, _, b in probs], cst)
        for (p, sl, b), out, s_new in zip(probs, outs, s_news):
            s_ref[p, b] = s_new
            o_ref[b, :, sl] = out.astype(o_ref.dtype)
        return carry

    lax.fori_loop(0, r_ref.shape[0] // group, body, 0)


def _rwkv(acts, rk, gng, gnb, consts, *, group, pairs):
    b, s, rd = acts[0].shape
    c = RWKV_CHUNK
    tri, lvl, ones2 = consts
    act = pl.BlockSpec((b, c, pairs * LANE), lambda p, t: (0, t, p))
    par = pl.BlockSpec((1, pairs * LANE), lambda p, t: (0, p))
    return pl.pallas_call(
        functools.partial(_rwkv_kernel, group=group),
        grid=(rd // (pairs * LANE), s // c),
        in_specs=[act] * 7 + [par] * 3 + [_const_spec(tri.shape), _const_spec(lvl.shape),
                                          _const_spec(ones2.shape)],
        out_specs=act,
        out_shape=jax.ShapeDtypeStruct((b, s, rd), BF16),
        scratch_shapes=[pltpu.VMEM((pairs, b, LANE, LANE), F32)],
        compiler_params=pltpu.CompilerParams(
            dimension_semantics=("parallel", "arbitrary"), vmem_limit_bytes=VMEM_LIMIT_BYTES),
        name="rwkv",
    )(*acts, rk, gng, gnb, tri, lvl, ones2)


def _attn_kernel(q_ref, k_ref, v_ref, qm_ref, o_ref, s_scr, mx_scr, acc_scr, m_scr,
                 alpha_scr, *, tq, tk, rb):
    nq = q_ref.shape[1] // tq
    ratio = tk // tq
    n_steps = sum(i // ratio + 1 for i in range(nq))
    lane = lax.broadcasted_iota(jnp.int32, (1, LANE), 1)
    heads = range(2)
    hs = lambda h: slice(h * LANE, (h + 1) * LANE)

    def advance(ij):
        i, j = ij
        last = j == i // ratio
        ni = jnp.where(last, jnp.minimum(i + 1, nq - 1), i)
        nj = jnp.where(last, jnp.where(i == nq - 1, j, 0), j + 1)
        return ni, nj

    def lane_block_max(sc):
        mx = sc[:, :LANE]
        for c0 in range(LANE, tk, LANE):
            mx = jnp.maximum(mx, sc[:, c0:c0 + LANE])
        return mx

    def scores(ij, slot, heads=heads):
        i, j = ij
        q0 = pl.multiple_of(i * tq, tq)
        k0 = pl.multiple_of(j * tk, tk)
        qm = qm_ref[jnp.where(j == i // ratio, 1 + i % ratio, 0)]
        for h in heads:
            sc = _dot_nt(q_ref[0, pl.ds(q0, tq), hs(h)] + qm, k_ref[0, pl.ds(k0, tk), hs(h)])
            s_scr[slot, h] = sc
            mx_scr[slot, h] = lane_block_max(sc)

    def softmax(ij, slot, heads=heads):
        j = ij[1]
        cols = [pl.ds(c0, LANE) for c0 in range(0, tk, LANE)]
        ps = []
        for h in heads:
            blocks = []
            for r0 in range(0, tq, rb):
                rows = pl.ds(r0, rb)
                mx = jnp.broadcast_to(
                    jnp.max(mx_scr[slot, h, rows, :], axis=1, keepdims=True), (rb, LANE))
                m_old = jnp.where(j == 0, -jnp.inf, m_scr[h, rows, :])
                m_new = jnp.maximum(m_old, mx)
                blocks.append(jnp.concatenate(
                    [jnp.exp2(s_scr[slot, h, rows, cs] - m_new).astype(BF16) for cs in cols], axis=1))
                m_scr[h, rows, :] = m_new
                alpha_scr[h, rows, :] = jnp.exp2(m_old - m_new)
            ps.append(jnp.concatenate(blocks, axis=0))
        return ps

    def values(ij, ps, heads=heads):
        k0 = pl.multiple_of(ij[1] * tk, tk)
        for h, p in zip(heads, ps):
            acc_scr[h] = (alpha_scr[h] * acc_scr[h]
                          + _dot(p, v_ref[0, pl.ds(k0, tk), hs(h)]))

    def finalize(i):
        q0 = pl.multiple_of(i * tq, tq)
        a0, a1 = acc_scr[0], acc_scr[1]
        out = jnp.where(lane < HEAD, a0 / a0[:, HEAD:HEAD + 1], a1 / a1[:, 0:1])
        o_ref[0, pl.ds(q0, tq), :] = out.astype(o_ref.dtype)

    acc_scr[...] = jnp.zeros_like(acc_scr)
    m_scr[...] = jnp.full_like(m_scr, -jnp.inf)
    zero = jnp.int32(0)
    ij0 = (zero, zero)
    scores(ij0, 0)

    def step(carry, slot):
        ij_a, ij_b = carry
        for h in heads:
            scores(ij_b, 1 - slot, (h,))
            values(ij_a, softmax(ij_a, slot, (h,)), (h,))

        finalize(ij_a[0])
        return ij_b, advance(ij_b)

    carry = (ij0, advance(ij0))
    slot = 0
    for _ in range(n_steps % 4):
        carry = step(carry, slot)
        slot = 1 - slot

    def trip(_, c):
        for u in range(4):
            c = step(c, (slot + u) % 2)
        return c

    lax.fori_loop(0, n_steps // 4, trip, carry)


def _attn(q, k, v, qm, *, tq, tk):
    b, s, hq = q.shape
    pairs = hq // (2 * LANE)
    blk = pl.BlockSpec((1, s, 2 * LANE), lambda i, p: (i, 0, p))
    return pl.pallas_call(
        functools.partial(_attn_kernel, tq=tq, tk=tk, rb=min(64, tq)),
        grid=(b, pairs),
        in_specs=[blk, blk, blk, _const_spec(qm.shape)],
        out_specs=pl.BlockSpec((1, s, LANE), lambda i, p: (i, 0, p)),
        out_shape=jax.ShapeDtypeStruct((b, s, pairs * LANE), BF16),
        scratch_shapes=[pltpu.VMEM((2, 2, tq, tk), F32), pltpu.VMEM((2, 2, tq, LANE), F32),
                        pltpu.VMEM((2, tq, LANE), F32), pltpu.VMEM((2, tq, LANE), F32),
                        pltpu.VMEM((2, tq, LANE), F32)],
        compiler_params=pltpu.CompilerParams(
            dimension_semantics=("parallel", "parallel"), vmem_limit_bytes=VMEM_LIMIT_BYTES),
        name="attn",
    )(q, k, v, qm)


def _mix_ffn_ln_kernel(h_ref, yr_ref, om_ref, wgate_ref, wur_ref, wum_ref, wo_ref, g2_ref, b2_ref,
                       w1_ref, w3_ref, w2_ref, g3_ref, b3_ref, o_ref, *, alpha):
    h = h_ref[...]
    d = h.shape[1]
    gates = jax.nn.sigmoid(_dot(h.astype(BF16), wgate_ref[...]))
    y_r = _dot(yr_ref[...], wur_ref[...])
    y_m = _dot(om_ref[...], wum_ref[...])
    mixin = (gates[:, :d] * y_r + gates[:, d:] * y_m).astype(BF16)
    mix = _dot(mixin, wo_ref[...])
    h2 = _layer_norm(alpha * h + mix, g2_ref[...], b2_ref[...])
    o_ref[...] = _swiglu_ln(h2, w1_ref, w3_ref, w2_ref, g3_ref, b3_ref, alpha)


def _mix_ffn_ln(h2, yr, om, wts, *, alpha, tm):
    t, d = h2.shape
    tok = lambda n: pl.BlockSpec((tm, n), lambda i: (i, 0))
    return pl.pallas_call(
        functools.partial(_mix_ffn_ln_kernel, alpha=alpha),
        grid=(t // tm,),
        in_specs=[tok(d), tok(yr.shape[1]), tok(om.shape[1])] + [_const_spec(w.shape) for w in wts],
        out_specs=tok(d),
        out_shape=jax.ShapeDtypeStruct((t, d), F32),
        compiler_params=pltpu.CompilerParams(
            dimension_semantics=("parallel",), vmem_limit_bytes=VMEM_LIMIT_BYTES),
        name="mix_ffn_ln",
    )(h2, yr, om, *wts)


def _pad_cols(w, n):
    return jnp.pad(w, ((0, 0), (0, n - w.shape[1])))


def _rope_tables(s):
    inv_freq = ROPE_THETA ** (-jnp.arange(0, ROPE, 2, dtype=F32) / ROPE)
    ang = jnp.arange(s, dtype=F32)[:, None] * inv_freq[None, :]
    cos, sin = jnp.cos(ang), jnp.sin(ang)
    half = ROPE // 2
    z = lambda n: jnp.zeros((s, n), F32)
    cos_t = jnp.concatenate([jnp.ones((s, HEAD), F32), cos, cos, z(LANE - HEAD - ROPE)], axis=1)
    sin_a = jnp.concatenate([z(HEAD + half), sin, z(LANE - HEAD - ROPE)], axis=1)
    sin_b = jnp.concatenate([z(HEAD), -sin, z(LANE - HEAD - half)], axis=1)
    return cos_t, sin_a, sin_b


def _rwkv_consts():
    c = RWKV_CHUNK
    i = jnp.arange(c)[:, None]
    j = jnp.arange(c)[None, :]
    tri = (i >= j).astype(BF16)
    lv = []
    for lg in range(RWKV_LEVELS):
        m = ((i >> (lg + 1)) == (j >> (lg + 1))) & (((i >> lg) & 1) == 1) & (((j >> lg) & 1) == 0)
        lv.append(jnp.concatenate([m, m], axis=1).astype(F32))
    return tri, jnp.stack(lv)


def _ones2():
    i = jnp.arange(2 * LANE)[:, None] % LANE
    j = jnp.arange(LANE)[None, :]
    return ((i // HEAD) == (j // HEAD)).astype(BF16)


def _mask_tables(s, tq, tk):
    nc = tk // CHUNK
    assert MASK_LANE0 + nc <= LANE
    lane = jnp.arange(LANE)[None, :] - MASK_LANE0
    kc = (jnp.arange(s)[:, None] % tk) // CHUNK
    kmask = jnp.where((lane >= 0) & (lane < nc) & (kc > lane), MASK_NEG, 0.0).astype(F32)
    qm = [jnp.zeros((tq, LANE), F32)]
    for v in range(tk // tq):
        qc = (v * tq + jnp.arange(tq)[:, None]) // CHUNK
        qm.append((lane == qc).astype(F32))
    return kmask, jnp.stack(qm).astype(BF16)


def kernel(x, ffn1_w1, ffn1_w3, ffn1_w2, ln1_g, ln1_b, w_in, mu_shift, w0, w_decay_up, a0, w_iclr_up, w_gate_up, k_k, k_a, r_k, gn_g, gn_b, q_norm_g, w_q_up, kv_norm_g, w_kv_up, w_up_rwkv, w_up_mla, w_o, ln2_g, ln2_b, ffn2_w1, ffn2_w3, ffn2_w2, ln3_g, ln3_b):
    b, s, d = x.shape
    depth = ffn1_w1.shape[0]
    rd = w0.shape[1]
    nh = d // LANE
    alpha = float((2.0 * depth) ** 0.25)
    f = ffn1_w1.shape[2]
    fp = -(-f // LANE) * LANE
    tm_ffn = min(256, b * s)
    tm_proj = min(512, s)
    tq = min(512, s)
    tk = min(2 * tq, s)
    row = lambda p: p.reshape(1, -1).astype(F32)

    kmask, qm = _mask_tables(s, tq, tk)
    tabs = _rope_tables(s) + (kmask,)
    tri, lvl = _rwkv_consts()
    ones2 = _ones2()
    lane = jnp.arange(nh * LANE) % LANE
    head = jnp.arange(nh * LANE) // LANE
    vone = jnp.where(head % 2 == 0, lane == HEAD, lane == 0).astype(F32)[None, :]

    n_rkv = 3 * rd
    n_lora = n_rkv + DECAY_LORA + ICLR_LORA
    n_shift = n_lora + GATE_LORA
    n_shift_p = n_lora + 2 * LANE
    n_mla = n_shift + Q_LORA + KV_LORA

    h = x.reshape(b * s, d)
    for l in range(depth):
        ffn_wts = lambda w1, w3, w2, g, bb: (
            _pad_cols(w1[l], fp).astype(BF16), _pad_cols(w3[l], fp).astype(BF16),
            jnp.pad(w2[l], ((0, fp - f), (0, 0))).astype(BF16), row(g[l]), row(bb[l]))
        h = _ffn_ln(h, *ffn_wts(ffn1_w1, ffn1_w3, ffn1_w2, ln1_g, ln1_b), alpha=alpha,
                    tm=min(512, b * s))

        w = w_in[l]
        z = lambda n: jnp.zeros((d, n), F32)
        kpe_blk = jnp.concatenate([z(HEAD), w[:, n_mla:n_mla + ROPE], z(LANE - HEAD - ROPE)], axis=1)
        win = jnp.concatenate([w[:, :n_shift], z(n_shift_p - n_shift), w[:, n_shift:n_mla], kpe_blk],
                              axis=1).astype(BF16)
        mu = jnp.pad(mu_shift[l], (0, n_shift_p - n_shift))[None, :]
        zl = jnp.zeros((DECAY_LORA, rd), F32)
        wda = jnp.concatenate([jnp.concatenate([w_decay_up[l], zl], axis=1),
                               jnp.concatenate([zl, w_iclr_up[l]], axis=1)], axis=0).astype(BF16)
        wg = jnp.pad(w_gate_up[l], ((0, 2 * LANE - GATE_LORA), (0, 0))).astype(BF16)
        wq = jnp.pad(w_q_up[l].reshape(Q_LORA, nh, QK_HEAD),
                     ((0, 0), (0, 0), (0, LANE - QK_HEAD))).reshape(Q_LORA, nh * LANE).astype(BF16)
        wkv3 = w_kv_up[l].reshape(KV_LORA, nh, 2 * HEAD)
        zk = jnp.zeros((KV_LORA, nh, HEAD), F32)
        wk = jnp.concatenate([wkv3[:, :, :HEAD], zk], axis=2).reshape(KV_LORA, nh * LANE)
        wv_even = jnp.concatenate([wkv3[:, :, HEAD:], zk], axis=2)
        wv_odd = jnp.concatenate([zk, wkv3[:, :, HEAD:]], axis=2)
        odd = (jnp.arange(nh) % 2 == 1)[None, :, None]
        wv = jnp.where(odd, wv_odd, wv_even).reshape(KV_LORA, nh * LANE)
        wkv = jnp.concatenate([wk, wv], axis=1).astype(BF16)
        wts = (win, mu, wda, row(w0[l]), row(a0[l]), wg, row(k_k[l]), row(k_a[l]), ones2,
               row(q_norm_g[l]), wq, row(kv_norm_g[l]), wkv, vone)
        r, ld, km, v, kk, a, g, q, kx, vx = _proj(h.reshape(b, s, d), tabs, wts, tm=tm_proj)

        yr = _rwkv((r, ld, km, v, kk, a, g), row(r_k[l]), row(gn_g[l]), row(gn_b[l]),
                   (tri, lvl, ones2), group=min(8, b), pairs=2)
        om = _attn(q, kx, vx, qm, tq=tq, tk=tk)

        mix_wts = (w[:, n_mla + ROPE:].astype(BF16), w_up_rwkv[l].astype(BF16),
                   w_up_mla[l].astype(BF16), w_o[l].astype(BF16), row(ln2_g[l]), row(ln2_b[l]))
        h = _mix_ffn_ln(h, yr.reshape(b * s, rd), om.reshape(b * s, -1),
                        mix_wts + ffn_wts(ffn2_w1, ffn2_w3, ffn2_w2, ln3_g, ln3_b),
                        alpha=alpha, tm=tm_ffn)
    return h.reshape(b, s, d)
```

```python
import functools

import jax
import jax.numpy as jnp
from jax import lax
from jax.experimental import pallas as pl
from jax.experimental.pallas import tpu as pltpu

F32 = jnp.float32
BF16 = jnp.bfloat16

LANE = 128
SUBLANE = 8
VMEM_LIMIT_BYTES = 56 * 1024 * 1024

HEAD = 64
ROPE = 32
QK_HEAD = HEAD + ROPE
CHUNK = 64
DECAY_LORA = 64
ICLR_LORA = 64
GATE_LORA = 160
Q_LORA = 384
KV_LORA = 256
GN_EPS = 64e-5
LN_EPS = 1e-5
RMS_EPS = 1e-6
ROPE_THETA = 10000.0
LOG2E = 1.4426950408889634

MASK_LANE0 = HEAD + ROPE
MASK_NEG = -1e30

RWKV_CHUNK = 128
RWKV_LEVELS = 7


def _dot(a, b):
    return jnp.dot(a, b, preferred_element_type=F32)


def _dot_nt(a, b):
    return lax.dot_general(a, b, (((1,), (1,)), ((), ())), preferred_element_type=F32)


def _dot_tn(a, b):
    return lax.dot_general(a, b, (((0,), (0,)), ((), ())), preferred_element_type=F32)


def _split2(x):
    hi = x.astype(BF16)
    lo = (x - hi.astype(F32)).astype(BF16)
    return hi, lo


def _split3(x):
    hi = x.astype(BF16)
    r1 = x - hi.astype(F32)
    mid = r1.astype(BF16)
    lo = (r1 - mid.astype(F32)).astype(BF16)
    return hi, mid, lo


def _layer_norm(z, g, b):
    mu = jnp.mean(z, axis=-1, keepdims=True)
    zc = z - mu
    var = jnp.mean(zc * zc, axis=-1, keepdims=True)
    return zc * lax.rsqrt(var + LN_EPS) * g + b


def _rms_norm(z, g):
    return z * lax.rsqrt(jnp.mean(z * z, axis=-1, keepdims=True) + RMS_EPS) * g


def _softplus(u):
    return jnp.maximum(u, 0.0) + jnp.log(1.0 + jnp.exp(-jnp.abs(u)))


def _const_spec(shape):
    nd = len(shape)
    return pl.BlockSpec(shape, lambda *_: (0,) * nd, pipeline_mode=pl.Buffered(1))


def _swiglu_ln(x, w1_ref, w3_ref, w2_ref, g_ref, b_ref, alpha):
    xb = x.astype(BF16)
    h1 = _dot(xb, w1_ref[...])
    h3 = _dot(xb, w3_ref[...])
    act = (h1 * jax.nn.sigmoid(h1) * h3).astype(BF16)
    y = _dot(act, w2_ref[...])
    return _layer_norm(alpha * x + 0.5 * y, g_ref[...], b_ref[...])


def _ffn_ln_kernel(x_ref, w1_ref, w3_ref, w2_ref, g_ref, b_ref, o_ref, *, alpha):
    o_ref[...] = _swiglu_ln(x_ref[...], w1_ref, w3_ref, w2_ref, g_ref, b_ref, alpha)


def _ffn_ln(x2, w1, w3, w2, g, b, *, alpha, tm):
    t, d = x2.shape
    f = w1.shape[1]
    return pl.pallas_call(
        functools.partial(_ffn_ln_kernel, alpha=alpha),
        grid=(t // tm,),
        in_specs=[
            pl.BlockSpec((tm, d), lambda i: (i, 0)),
            _const_spec((d, f)), _const_spec((d, f)), _const_spec((f, d)),
            _const_spec((1, d)), _const_spec((1, d)),
        ],
        out_specs=pl.BlockSpec((tm, d), lambda i: (i, 0)),
        out_shape=jax.ShapeDtypeStruct((t, d), F32),
        compiler_params=pltpu.CompilerParams(
            dimension_semantics=("parallel",), vmem_limit_bytes=VMEM_LIMIT_BYTES),
        name="ffn_ln",
    )(x2, w1, w3, w2, g, b)


def _segsum64(x, ones2):
    outs = []
    for c in range(x.shape[1] // LANE):
        hi, lo = _split2(x[:, c * LANE:(c + 1) * LANE])
        outs.append(_dot(jnp.concatenate([hi, lo], axis=1), ones2))
    return outs[0] if len(outs) == 1 else jnp.concatenate(outs, axis=1)


def _rope(x, cos_t, sin_a, sin_b):
    return (x * cos_t + pltpu.roll(x, 16, axis=1) * sin_a
            + pltpu.roll(x, LANE - 16, axis=1) * sin_b)


def _proj_kernel(h_ref, cos_ref, sa_ref, sb_ref, km_ref, win_ref, mu_ref, wda_ref, w0_ref, a0_ref,
                 wg_ref, kk_ref, ka_ref, ones2_ref, qg_ref, wq_ref, kvg_ref, wkv_ref, vone_ref,
                 r_o, ld_o, k_o, v_o, kk_o, a_o, g_o, q_o, kx_o, vx_o, carry_ref,
                 *, rd, n_shift, q_scale):
    s = pl.program_id(1)
    tm = h_ref.shape[1]
    hb = h_ref[0].astype(BF16)
    proj = _dot(hb, win_ref[...])

    ps = proj[:, :n_shift]
    prev = pltpu.roll(ps, 1, axis=0)
    first = jnp.where(s == 0, 0.0, carry_ref[SUBLANE - 1:SUBLANE, :])
    rowid = lax.broadcasted_iota(jnp.int32, (tm, 1), 0)
    prev = jnp.where(rowid == 0, first, prev)
    carry_ref[...] = ps[tm - SUBLANE:, :]
    ps = ps + mu_ref[...] * (prev - ps)

    r = ps[:, 0:rd]
    k = ps[:, rd:2 * rd]
    v = ps[:, 2 * rd:3 * rd]
    wa = ps[:, 3 * rd:3 * rd + LANE]
    gd = ps[:, 3 * rd + LANE:n_shift]

    lane = lax.broadcasted_iota(jnp.int32, (1, LANE), 1)
    t_in = jnp.where(lane < DECAY_LORA, jnp.tanh(wa), wa).astype(BF16)
    za = _dot(t_in, wda_ref[...])
    z = w0_ref[...] + za[:, :rd]
    w = -_softplus(-z) - 0.5
    ld_o[0] = -jnp.exp(w)
    a = jax.nn.sigmoid(a0_ref[...] + za[:, rd:])
    g_o[0] = _dot(jax.nn.sigmoid(gd).astype(BF16), wg_ref[...])

    kkr = k * kk_ref[...]
    ss = _segsum64(kkr * kkr, ones2_ref[...])
    kk_o[0] = kkr * lax.rsqrt(jnp.maximum(ss, 1e-24))
    k_o[0] = k * (1.0 + (a - 1.0) * ka_ref[...])
    r_o[0] = r
    v_o[0] = v
    a_o[0] = a

    pm = proj[:, n_shift:]
    q_lat = pm[:, :Q_LORA]
    kv_lat = pm[:, Q_LORA:Q_LORA + KV_LORA]
    kp = pm[:, Q_LORA + KV_LORA:]
    cos_t, sin_a, sin_b = cos_ref[...], sa_ref[...], sb_ref[...]
    q = _dot(_rms_norm(q_lat, qg_ref[...]).astype(BF16), wq_ref[...])
    kv = _dot(_rms_norm(kv_lat, kvg_ref[...]).astype(BF16), wkv_ref[...])
    kpr = _rope(kp, cos_t, sin_a, sin_b) + km_ref[...]
    nh = q.shape[1] // LANE
    for h in range(nh):
        sl = slice(h * LANE, (h + 1) * LANE)
        q_o[0, :, sl] = (_rope(q[:, sl], cos_t, sin_a, sin_b) * q_scale).astype(BF16)
        kx_o[0, h] = (kv[:, sl] + kpr).T.astype(BF16)
    vx_o[0] = (kv[:, nh * LANE:] + vone_ref[...]).astype(BF16)


def _proj(h3, tabs, wts, *, tm):
    b, s, d = h3.shape
    (win, mu, wda, w0, a0, wg, kkw, kaw, ones2, qg, wq, kvg, wkv, vone) = wts
    rd = w0.shape[1]
    n_shift = mu.shape[1]
    hq = wq.shape[1]
    tok = lambda n: pl.BlockSpec((1, tm, n), lambda i, j: (i, j, 0))
    tab = pl.BlockSpec((tm, LANE), lambda i, j: (j, 0))
    f32o = jax.ShapeDtypeStruct((b, s, rd), F32)
    bfo = jax.ShapeDtypeStruct((b, s, hq), BF16)
    return pl.pallas_call(
        functools.partial(_proj_kernel, rd=rd, n_shift=n_shift,
                          q_scale=float(QK_HEAD ** -0.5 * LOG2E)),
        grid=(b, s // tm),
        in_specs=[tok(d), tab, tab, tab, tab] + [_const_spec(w.shape) for w in wts],
        out_specs=[tok(rd)] * 7 + [tok(hq), pl.BlockSpec((1, hq // LANE, LANE, tm),
                                                         lambda i, j: (i, 0, 0, j)), tok(hq)],
        out_shape=[f32o] * 7 + [bfo, jax.ShapeDtypeStruct((b, hq // LANE, LANE, s), BF16), bfo],
        scratch_shapes=[pltpu.VMEM((SUBLANE, n_shift), F32)],
        compiler_params=pltpu.CompilerParams(
            dimension_semantics=("parallel", "arbitrary"), vmem_limit_bytes=VMEM_LIMIT_BYTES),
        name="proj",
    )(h3, *tabs, *wts)


def _wide_dot(a_w, b_w):
    c = a_w.shape[0]
    z = jnp.zeros((c, c), b_w.dtype)
    rhs = jnp.concatenate([jnp.concatenate([b_w[:, :c], z], axis=1),
                           jnp.concatenate([z, b_w[:, c:]], axis=1)], axis=0)
    return _dot(a_w, rhs)


def _rows(x, n):
    m = x.shape[0] // n
    return [x[i * m:(i + 1) * m] for i in range(n)]


def _segsum64_stacked(xs, ones2):
    parts = [jnp.concatenate(_split2(x), axis=1) for x in xs]
    return _rows(_dot(jnp.concatenate(parts, axis=0), ones2), len(xs))


def _rwkv_chunks(xs, s_prevs, cst):
    (tri, eye_w, lvl_ref, strict_w, incl_w, bd, ones2, m0, m1) = cst
    c = RWKV_CHUNK
    half = c // 2
    n = len(xs)
    each = lambda f, *ls: [f(*a) for a in zip(*ls)]
    rs, lds, ks, vs, kks, as_, gs, rks, gngs, gnbs = (list(t) for t in zip(*xs))

    ld3 = jnp.concatenate([jnp.concatenate(_split3(ld), axis=1) for ld in lds], axis=1)
    cl3 = _dot(tri, ld3)
    cls = [cl3[:, (3 * i) * LANE:(3 * i + 1) * LANE] + cl3[:, (3 * i + 1) * LANE:(3 * i + 2) * LANE]
           + cl3[:, (3 * i + 2) * LANE:(3 * i + 3) * LANE] for i in range(n)]
    refs = [cl[half - 1:half, :] for cl in cls]
    es = each(lambda cl, ref: jnp.exp(cl - ref), cls, refs)
    einvs = each(lambda cl, ref: jnp.exp(ref - cl), cls, refs)
    rts = each(lambda r, e: r * e, rs, es)
    ats = each(lambda kk, e, ld: -(kk * e) * jnp.exp(-ld), kks, es, lds)
    bts = each(lambda kk, a, ei: kk * a * ei, kks, as_, einvs)
    kts = each(lambda k, ei: k * ei, ks, einvs)

    lhs4s = each(lambda at, rt: jnp.concatenate([at * m0, at * m1, rt * m0, rt * m1],
                                                axis=0).astype(BF16), ats, rts)
    rhs2s = each(lambda bt, kt: jnp.concatenate([bt, kt], axis=0).astype(BF16), bts, kts)
    gms = each(_dot_nt, lhs4s, rhs2s)
    pair = lambda gm, r0, c0: jnp.concatenate(
        [gm[r0:r0 + c, c0:c0 + c], gm[r0 + c:r0 + 2 * c, c0:c0 + c]], axis=1)
    lab_ws = [jnp.where(strict_w, pair(gm, 0, 0), 0.0) for gm in gms]
    lak_ws = [jnp.where(strict_w, pair(gm, 0, c), 0.0) for gm in gms]
    arb_ws = [jnp.where(incl_w, pair(gm, 2 * c, 0), 0.0) for gm in gms]
    ark_ws = [jnp.where(incl_w, pair(gm, 2 * c, c), 0.0) for gm in gms]

    top = lax.broadcasted_iota(jnp.int32, (half, 2 * c), 1) % c < half
    quad = (lax.broadcasted_iota(jnp.int32, (2 * c, 2 * c), 0) // half
            == lax.broadcasted_iota(jnp.int32, (2 * c, 2 * c), 1) // half)
    pack = lambda w: jnp.where(top, w[:half], w[half:])
    quad_dot = lambda a, b: _dot(a, jnp.where(quad, jnp.concatenate([b] * 4, axis=0), 0))
    lvl_ps = [pack(lvl_ref[lv]) for lv in range(RWKV_LEVELS - 1)]
    lab_ps = [pack(lab) for lab in lab_ws]
    eye_p = pack(eye_w)
    t_ps = [eye_p + lab * lvl_ps[0] for lab in lab_ps]
    for lv in range(1, RWKV_LEVELS - 1):
        lcs = [(lab * lvl_ps[lv]).astype(BF16) for lab in lab_ps]
        tbs = [t.astype(BF16) for t in t_ps]
        xps = each(quad_dot, lcs, tbs)
        tx = each(lambda tb, xp: quad_dot(tb, xp.astype(BF16)), tbs, xps)
        t_ps = each(lambda t, d: t + d, t_ps, tx)
    t_ws = [jnp.concatenate([jnp.where(top, t, 0.0), jnp.where(top, 0.0, t)], axis=0) for t in t_ps]
    lcs = [(lab * lvl_ref[RWKV_LEVELS - 1]).astype(BF16) for lab in lab_ws]
    tbs = [t.astype(BF16) for t in t_ws]
    xws = each(_wide_dot, lcs, tbs)
    tx = each(lambda tb, xw: _wide_dot(tb, xw.astype(BF16)), tbs, xws)
    t_ws = each(lambda t, d: t + d, t_ws, tx)

    v01s = [jnp.concatenate([v * m0, v * m1], axis=0).astype(BF16) for v in vs]
    xvs = each(lambda lak, v01: _dot(lak.astype(BF16), v01), lak_ws, v01s)
    rhss = each(lambda at, xv: jnp.concatenate(
        [jnp.concatenate([at * m0, xv * m0], axis=1),
         jnp.concatenate([at * m1, xv * m1], axis=1)], axis=0).astype(BF16), ats, xvs)
    wus = each(lambda t, rhs: _dot(t.astype(BF16), rhs), t_ws, rhss)
    sps = each(lambda s, ref: s * jnp.exp(ref), s_prevs, refs)
    wss = each(lambda wu, rt, sp: _dot_nt(
        jnp.concatenate([wu[:, :LANE], rt], axis=0).astype(BF16), sp.astype(BF16)), wus, rts, sps)
    us = each(lambda ws, wu: ws[:c] + wu[:, LANE:], wss, wus)
    ys = each(lambda ws, arb, ark, u, v01: ws[c:] + _dot(
        jnp.concatenate([arb, ark], axis=1).astype(BF16),
        jnp.concatenate([(u * m0).astype(BF16), (u * m1).astype(BF16), v01], axis=0)),
        wss, arb_ws, ark_ws, us, v01s)
    upds = each(lambda u, v, rhs2: _dot_tn(jnp.concatenate([u, v], axis=0).astype(BF16), rhs2),
                us, vs, rhs2s)
    s_news = each(lambda sp, upd, e: (sp + jnp.where(bd, upd, 0.0)) * e[c - 1:c, :],
                  sps, upds, es)

    mus = _segsum64_stacked(ys, ones2)
    ycs = each(lambda y, mu: y - mu * (1.0 / HEAD), ys, mus)
    vars_ = _segsum64_stacked([yc * yc for yc in ycs], ones2)
    bsums = _segsum64_stacked(each(lambda r, k, rk: r * k * rk, rs, ks, rks), ones2)
    outs = each(lambda yc, var, bs, v, g, gng, gnb: (
        yc * lax.rsqrt(var * (1.0 / HEAD) + GN_EPS) * gng + gnb + bs * v) * g,
        ycs, vars_, bsums, vs, gs, gngs, gnbs)
    return outs, s_news


def _rwkv_kernel(r_ref, ld_ref, k_ref, v_ref, kk_ref, a_ref, g_ref, rk_ref, gng_ref, gnb_ref,
                 tri_ref, lvl_ref, ones2_ref, o_ref, s_ref, *, group):
    c = RWKV_CHUNK

    @pl.when(pl.program_id(1) == 0)
    def _():
        s_ref[...] = jnp.zeros_like(s_ref)

    row = lax.broadcasted_iota(jnp.int32, (c, 2 * c), 0)
    colw = lax.broadcasted_iota(jnp.int32, (c, 2 * c), 1) & (c - 1)
    strict_w = row > colw
    incl_w = row >= colw
    eye_w = (row == colw).astype(F32)
    r2 = lax.broadcasted_iota(jnp.int32, (LANE, LANE), 0)
    c2 = lax.broadcasted_iota(jnp.int32, (LANE, LANE), 1)
    bd = (r2 >= HEAD) == (c2 >= HEAD)
    lane = lax.broadcasted_iota(jnp.int32, (1, LANE), 1)
    m0 = (lane < HEAD).astype(F32)
    m1 = 1.0 - m0
    cst = (tri_ref[...], eye_w, lvl_ref, strict_w, incl_w, bd, ones2_ref[...], m0, m1)
    in_refs = (r_ref, ld_ref, k_ref, v_ref, kk_ref, a_ref, g_ref)
    par_refs = (rk_ref, gng_ref, gnb_ref)
    pairs = [slice(p * LANE, (p + 1) * LANE) for p in range(r_ref.shape[2] // LANE)]

    def body(i, carry):
        probs = [(p, sl, i * group + j) for p, sl in enumerate(pairs) for j in range(group)]
        outs, s_news = _rwkv_chunks(
            [tuple(ref[b, :, sl] for ref in in_refs) + tuple(ref[:, sl] for ref in par_refs)
             for _, sl, b in probs],
            [s_ref[p, b] for p, _, b in probs], cst)
        for (p, sl, b), out, s_new in zip(probs, outs, s_news):
            s_ref[p, b] = s_new
            o_ref[b, :, sl] = out.astype(o_ref.dtype)
        return carry

    lax.fori_loop(0, r_ref.shape[0] // group, body, 0)


def _rwkv(acts, rk, gng, gnb, consts, *, group, pairs):
    b, s, rd = acts[0].shape
    c = RWKV_CHUNK
    tri, lvl, ones2 = consts
    act = pl.BlockSpec((b, c, pairs * LANE), lambda p, t: (0, t, p))
    par = pl.BlockSpec((1, pairs * LANE), lambda p, t: (0, p))
    return pl.pallas_call(
        functools.partial(_rwkv_kernel, group=group),
        grid=(rd // (pairs * LANE), s // c),
        in_specs=[act] * 7 + [par] * 3 + [_const_spec(tri.shape), _const_spec(lvl.shape),
                                          _const_spec(ones2.shape)],
        out_specs=act,
        out_shape=jax.ShapeDtypeStruct((b, s, rd), BF16),
        scratch_shapes=[pltpu.VMEM((pairs, b, LANE, LANE), F32)],
        compiler_params=pltpu.CompilerParams(
            dimension_semantics=("parallel", "arbitrary"), vmem_limit_bytes=VMEM_LIMIT_BYTES),
        name="rwkv",
    )(*acts, rk, gng, gnb, tri, lvl, ones2)


def _attn_kernel(q_ref, k_ref, v_ref, qm_ref, o_ref, s_scr, mx_scr, acc_scr, m_scr,
                 alpha_scr, *, tq, tk, rb):
    nq = q_ref.shape[1] // tq
    ratio = tk // tq
    n_steps = sum(i // ratio + 1 for i in range(nq))
    lane = lax.broadcasted_iota(jnp.int32, (1, LANE), 1)
    heads = range(2)
    hs = lambda h: slice(h * LANE, (h + 1) * LANE)

    def advance(ij):
        i, j = ij
        last = j == i // ratio
        ni = jnp.where(last, jnp.minimum(i + 1, nq - 1), i)
        nj = jnp.where(last, jnp.where(i == nq - 1, j, 0), j + 1)
        return ni, nj

    def lane_block_max(sc):
        mx = sc[:, :LANE]
        for c0 in range(LANE, tk, LANE):
            mx = jnp.maximum(mx, sc[:, c0:c0 + LANE])
        return mx

    def scores(ij, slot, heads=heads):
        i, j = ij
        q0 = pl.multiple_of(i * tq, tq)
        k0 = pl.multiple_of(j * tk, tk)
        qm = qm_ref[jnp.where(j == i // ratio, 1 + i % ratio, 0)]
        for h in heads:
            sc = _dot(q_ref[0, pl.ds(q0, tq), hs(h)] + qm, k_ref[0, h, :, pl.ds(k0, tk)])
            s_scr[slot, h] = sc
            mx_scr[slot, h] = lane_block_max(sc)

    def softmax(ij, slot, heads=heads):
        j = ij[1]
        cols = [pl.ds(c0, LANE) for c0 in range(0, tk, LANE)]
        ps = []
        for h in heads:
            blocks = []
            for r0 in range(0, tq, rb):
                rows = pl.ds(r0, rb)
                mx = jnp.broadcast_to(
                    jnp.max(mx_scr[slot, h, rows, :], axis=1, keepdims=True), (rb, LANE))
                m_old = jnp.where(j == 0, -jnp.inf, m_scr[h, rows, :])
                m_new = jnp.maximum(m_old, mx)
                blocks.append(jnp.concatenate(
                    [jnp.exp2(s_scr[slot, h, rows, cs] - m_new).astype(BF16) for cs in cols], axis=1))
                m_scr[h, rows, :] = m_new
                alpha_scr[h, rows, :] = jnp.exp2(m_old - m_new)
            ps.append(jnp.concatenate(blocks, axis=0))
        return ps

    def values(ij, ps, heads=heads):
        k0 = pl.multiple_of(ij[1] * tk, tk)
        for h, p in zip(heads, ps):
            acc_scr[h] = (alpha_scr[h] * acc_scr[h]
                          + _dot(p, v_ref[0, pl.ds(k0, tk), hs(h)]))

    def finalize(i):
        q0 = pl.multiple_of(i * tq, tq)
        a0, a1 = acc_scr[0], acc_scr[1]
        out = jnp.where(lane < HEAD, a0 / a0[:, HEAD:HEAD + 1], a1 / a1[:, 0:1])
        o_ref[0, pl.ds(q0, tq), :] = out.astype(o_ref.dtype)

    acc_scr[...] = jnp.zeros_like(acc_scr)
    m_scr[...] = jnp.full_like(m_scr, -jnp.inf)
    zero = jnp.int32(0)
    ij0 = (zero, zero)
    scores(ij0, 0)

    def step(carry, slot):
        ij_a, ij_b = carry
        for h in heads:
            scores(ij_b, 1 - slot, (h,))
            values(ij_a, softmax(ij_a, slot, (h,)), (h,))

        finalize(ij_a[0])
        return ij_b, advance(ij_b)

    carry = (ij0, advance(ij0))
    slot = 0
    for _ in range(n_steps % 4):
        carry = step(carry, slot)
        slot = 1 - slot

    def trip(_, c):
        for u in range(4):
            c = step(c, (slot + u) % 2)
        return c

    lax.fori_loop(0, n_steps // 4, trip, carry)


def _attn(q, k, v, qm, *, tq, tk):
    b, s, hq = q.shape
    pairs = hq // (2 * LANE)
    blk = pl.BlockSpec((1, s, 2 * LANE), lambda i, p: (i, 0, p))
    return pl.pallas_call(
        functools.partial(_attn_kernel, tq=tq, tk=tk, rb=min(64, tq)),
        grid=(b, pairs),
        in_specs=[blk, pl.BlockSpec((1, 2, LANE, s), lambda i, p: (i, p, 0, 0)), blk,
                  _const_spec(qm.shape)],
        out_specs=pl.BlockSpec((1, s, LANE), lambda i, p: (i, 0, p)),
        out_shape=jax.ShapeDtypeStruct((b, s, pairs * LANE), BF16),
        scratch_shapes=[pltpu.VMEM((2, 2, tq, tk), F32), pltpu.VMEM((2, 2, tq, LANE), F32),
                        pltpu.VMEM((2, tq, LANE), F32), pltpu.VMEM((2, tq, LANE), F32),
                        pltpu.VMEM((2, tq, LANE), F32)],
        compiler_params=pltpu.CompilerParams(
            dimension_semantics=("parallel", "parallel"), vmem_limit_bytes=VMEM_LIMIT_BYTES),
        name="attn",
    )(q, k, v, qm)


def _mix_ffn_ln_kernel(h_ref, yr_ref, om_ref, wgate_ref, wur_ref, wum_ref, wo_ref, g2_ref, b2_ref,
                       w1_ref, w3_ref, w2_ref, g3_ref, b3_ref, o_ref, *, alpha):
    h = h_ref[...]
    d = h.shape[1]
    gates = jax.nn.sigmoid(_dot(h.astype(BF16), wgate_ref[...]))
    y_r = _dot(yr_ref[...], wur_ref[...])
    y_m = _dot(om_ref[...], wum_ref[...])
    mixin = (gates[:, :d] * y_r + gates[:, d:] * y_m).astype(BF16)
    mix = _dot(mixin, wo_ref[...])
    h2 = _layer_norm(alpha * h + mix, g2_ref[...], b2_ref[...])
    o_ref[...] = _swiglu_ln(h2, w1_ref, w3_ref, w2_ref, g3_ref, b3_ref, alpha)


def _mix_ffn_ln(h2, yr, om, wts, *, alpha, tm):
    t, d = h2.shape
    tok = lambda n: pl.BlockSpec((tm, n), lambda i: (i, 0))
    return pl.pallas_call(
        functools.partial(_mix_ffn_ln_kernel, alpha=alpha),
        grid=(t // tm,),
        in_specs=[tok(d), tok(yr.shape[1]), tok(om.shape[1])] + [_const_spec(w.shape) for w in wts],
        out_specs=tok(d),
        out_shape=jax.ShapeDtypeStruct((t, d), F32),
        compiler_params=pltpu.CompilerParams(
            dimension_semantics=("parallel",), vmem_limit_bytes=VMEM_LIMIT_BYTES),
        name="mix_ffn_ln",
    )(h2, yr, om, *wts)


def _pad_cols(w, n):
    return jnp.pad(w, ((0, 0), (0, n - w.shape[1])))


def _rope_tables(s):
    inv_freq = ROPE_THETA ** (-jnp.arange(0, ROPE, 2, dtype=F32) / ROPE)
    ang = jnp.arange(s, dtype=F32)[:, None] * inv_freq[None, :]
    cos, sin = jnp.cos(ang), jnp.sin(ang)
    half = ROPE // 2
    z = lambda n: jnp.zeros((s, n), F32)
    cos_t = jnp.concatenate([jnp.ones((s, HEAD), F32), cos, cos, z(LANE - HEAD - ROPE)], axis=1)
    sin_a = jnp.concatenate([z(HEAD + half), sin, z(LANE - HEAD - ROPE)], axis=1)
    sin_b = jnp.concatenate([z(HEAD), -sin, z(LANE - HEAD - half)], axis=1)
    return cos_t, sin_a, sin_b


def _rwkv_consts():
    c = RWKV_CHUNK
    i = jnp.arange(c)[:, None]
    j = jnp.arange(c)[None, :]
    tri = (i >= j).astype(BF16)
    lv = []
    for lg in range(RWKV_LEVELS):
        m = ((i >> (lg + 1)) == (j >> (lg + 1))) & (((i >> lg) & 1) == 1) & (((j >> lg) & 1) == 0)
        lv.append(jnp.concatenate([m, m], axis=1).astype(F32))
    return tri, jnp.stack(lv)


def _ones2():
    i = jnp.arange(2 * LANE)[:, None] % LANE
    j = jnp.arange(LANE)[None, :]
    return ((i // HEAD) == (j // HEAD)).astype(BF16)


def _mask_tables(s, tq, tk):
    nc = tk // CHUNK
    assert MASK_LANE0 + nc <= LANE
    lane = jnp.arange(LANE)[None, :] - MASK_LANE0
    kc = (jnp.arange(s)[:, None] % tk) // CHUNK
    kmask = jnp.where((lane >= 0) & (lane < nc) & (kc > lane), MASK_NEG, 0.0).astype(F32)
    qm = [jnp.zeros((tq, LANE), F32)]
    for v in range(tk // tq):
        qc = (v * tq + jnp.arange(tq)[:, None]) // CHUNK
        qm.append((lane == qc).astype(F32))
    return kmask, jnp.stack(qm).astype(BF16)


def kernel(x, ffn1_w1, ffn1_w3, ffn1_w2, ln1_g, ln1_b, w_in, mu_shift, w0, w_decay_up, a0, w_iclr_up, w_gate_up, k_k, k_a, r_k, gn_g, gn_b, q_norm_g, w_q_up, kv_norm_g, w_kv_up, w_up_rwkv, w_up_mla, w_o, ln2_g, ln2_b, ffn2_w1, ffn2_w3, ffn2_w2, ln3_g, ln3_b):
    b, s, d = x.shape
    depth = ffn1_w1.shape[0]
    rd = w0.shape[1]
    nh = d // LANE
    alpha = float((2.0 * depth) ** 0.25)
    f = ffn1_w1.shape[2]
    fp = -(-f // LANE) * LANE
    tm_ffn = min(256, b * s)
    tm_proj = min(512, s)
    tq = min(512, s)
    tk = min(2 * tq, s)
    row = lambda p: p.reshape(1, -1).astype(F32)

    kmask, qm = _mask_tables(s, tq, tk)
    tabs = _rope_tables(s) + (kmask,)
    tri, lvl = _rwkv_consts()
    ones2 = _ones2()
    lane = jnp.arange(nh * LANE) % LANE
    head = jnp.arange(nh * LANE) // LANE
    vone = jnp.where(head % 2 == 0, lane == HEAD, lane == 0).astype(F32)[None, :]

    n_rkv = 3 * rd
    n_lora = n_rkv + DECAY_LORA + ICLR_LORA
    n_shift = n_lora + GATE_LORA
    n_shift_p = n_lora + 2 * LANE
    n_mla = n_shift + Q_LORA + KV_LORA

    h = x.reshape(b * s, d)
    for l in range(depth):
        ffn_wts = lambda w1, w3, w2, g, bb: (
            _pad_cols(w1[l], fp).astype(BF16), _pad_cols(w3[l], fp).astype(BF16),
            jnp.pad(w2[l], ((0, fp - f), (0, 0))).astype(BF16), row(g[l]), row(bb[l]))
        h = _ffn_ln(h, *ffn_wts(ffn1_w1, ffn1_w3, ffn1_w2, ln1_g, ln1_b), alpha=alpha,
                    tm=min(512, b * s))

        w = w_in[l]
        z = lambda n: jnp.zeros((d, n), F32)
        kpe_blk = jnp.concatenate([z(HEAD), w[:, n_mla:n_mla + ROPE], z(LANE - HEAD - ROPE)], axis=1)
        win = jnp.concatenate([w[:, :n_shift], z(n_shift_p - n_shift), w[:, n_shift:n_mla], kpe_blk],
                              axis=1).astype(BF16)
        mu = jnp.pad(mu_shift[l], (0, n_shift_p - n_shift))[None, :]
        zl = jnp.zeros((DECAY_LORA, rd), F32)
        wda = jnp.concatenate([jnp.concatenate([w_decay_up[l], zl], axis=1),
                               jnp.concatenate([zl, w_iclr_up[l]], axis=1)], axis=0).astype(BF16)
        wg = jnp.pad(w_gate_up[l], ((0, 2 * LANE - GATE_LORA), (0, 0))).astype(BF16)
        wq = jnp.pad(w_q_up[l].reshape(Q_LORA, nh, QK_HEAD),
                     ((0, 0), (0, 0), (0, LANE - QK_HEAD))).reshape(Q_LORA, nh * LANE).astype(BF16)
        wkv3 = w_kv_up[l].reshape(KV_LORA, nh, 2 * HEAD)
        zk = jnp.zeros((KV_LORA, nh, HEAD), F32)
        wk = jnp.concatenate([wkv3[:, :, :HEAD], zk], axis=2).reshape(KV_LORA, nh * LANE)
        wv_even = jnp.concatenate([wkv3[:, :, HEAD:], zk], axis=2)
        wv_odd = jnp.concatenate([zk, wkv3[:, :, HEAD:]], axis=2)
        odd = (jnp.arange(nh) % 2 == 1)[None, :, None]
        wv = jnp.where(odd, wv_odd, wv_even).reshape(KV_LORA, nh * LANE)
        wkv = jnp.concatenate([wk, wv], axis=1).astype(BF16)
        wts = (win, mu, wda, row(w0[l]), row(a0[l]), wg, row(k_k[l]), row(k_a[l]), ones2,
               row(q_norm_g[l]), wq, row(kv_norm_g[l]), wkv, vone)
        r, ld, km, v, kk, a, g, q, kx, vx = _proj(h.reshape(b, s, d), tabs, wts, tm=tm_proj)

        yr = _rwkv((r, ld, km, v, kk, a, g), row(r_k[l]), row(gn_g[l]), row(gn_b[l]),
                   (tri, lvl, ones2), group=min(8, b), pairs=2)
        om = _attn(q, kx, vx, qm, tq=tq, tk=tk)

        mix_wts = (w[:, n_mla + ROPE:].astype(BF16), w_up_rwkv[l].astype(BF16),
                   w_up_mla[l].astype(BF16), w_o[l].astype(BF16), row(ln2_g[l]), row(ln2_b[l]))
        h = _mix_ffn_ln(h, yr.reshape(b * s, rd), om.reshape(b * s, -1),
                        mix_wts + ffn_wts(ffn2_w1, ffn2_w3, ffn2_w2, ln3_g, ln3_b),
                        alpha=alpha, tm=tm_ffn)
    return h.reshape(b, s, d)
```
